```python
import math
import jax, jax.numpy as jnp
from jax import lax
import numpy as np

D_MODEL = 4096
BATCH = 4
SEQ = 2048
DEPTH = 2

N_A_LAYERS = DEPTH // 2
N_B_LAYERS = DEPTH - N_A_LAYERS

A_HEADS = 16
A_HEAD_DIM = 128
A_WIDTH = A_HEADS * 2 * A_HEAD_DIM
A_QUERY_BLOCK = 128

B_HEADS = 32
B_HEAD_DIM = 128
B_WIDTH = B_HEADS * B_HEAD_DIM
MOBA_BLOCK = 256
MOBA_TOPK = 3
MOBA_Q_CHUNK = 8

REL_BUCKETS = 32
REL_MAX_DIST = 128
N_BIAS_COLS = 32

N_EXPERTS = 64
MOE_TOP_K = 8
EXPERT_FF = 256
SHARED_FF = 256
ROUTED_SCALE = 2.5
MOE_TOKEN_CHUNK_MAX = 2048

ADA_SCALE = 0.5
NORM_EPS = 1e-6
SUBLN_EPS = 1e-5

kernel_name = "yoco_diffattn_moba_moe_adaln"


def rms_norm(x, eps=NORM_EPS):
    xf = x.astype(jnp.float32)
    y = xf * lax.rsqrt(jnp.mean(xf * xf, axis=-1, keepdims=True) + eps)
    return y.astype(x.dtype)


def modulate(x, shift, scale):
    return rms_norm(x) * (1 + scale) + shift


def t5_bucket(dist):
    n = jnp.maximum(dist, 0)
    max_exact = REL_BUCKETS // 2
    nf = jnp.maximum(n, 1).astype(jnp.float32)
    large = max_exact + (jnp.log(nf / max_exact) / math.log(REL_MAX_DIST / max_exact)
                         * (REL_BUCKETS - max_exact)).astype(jnp.int32)
    large = jnp.minimum(large, REL_BUCKETS - 1)
    return jnp.where(n < max_exact, n, large)


def diff_attention(h, w_qkv, lam_vecs, subln_g, w_o, rel_bias, lambda_init):
    B, S, _ = h.shape
    q, k, v = jnp.split(h @ w_qkv, 3, axis=-1)
    q = q.reshape(B, S, 2 * A_HEADS, A_HEAD_DIM).transpose(0, 2, 1, 3)
    k = k.reshape(B, S, 2 * A_HEADS, A_HEAD_DIM).transpose(0, 2, 1, 3)
    v = v.reshape(B, S, A_HEADS, 2 * A_HEAD_DIM).transpose(0, 2, 1, 3)
    lv = lam_vecs.astype(jnp.float32)
    lam = jnp.exp(jnp.sum(lv[0] * lv[1])) - jnp.exp(jnp.sum(lv[2] * lv[3])) + lambda_init
    scale = A_HEAD_DIM ** -0.5
    rel_t = rel_bias.T.astype(jnp.float32)
    kpos = jnp.arange(S)
    nq = S // A_QUERY_BLOCK
    q_blocks = jnp.moveaxis(q.reshape(B, 2 * A_HEADS, nq, A_QUERY_BLOCK, A_HEAD_DIM), 2, 0)
    qpos_blocks = kpos.reshape(nq, A_QUERY_BLOCK)

    def block(args):
        qb, qpos = args
        dist = qpos[:, None] - kpos[None, :]
        logits = (jnp.einsum('bmqd,bmkd->bmqk', qb, k).astype(jnp.float32) * scale
                  + rel_t[:, t5_bucket(dist)][None])
        logits = jnp.where(dist >= 0, logits, -jnp.inf)
        p = jax.nn.softmax(logits, axis=-1).reshape(B, A_HEADS, 2, A_QUERY_BLOCK, S)
        diff = (p[:, :, 0] - lam * p[:, :, 1]).astype(v.dtype)
        return jnp.einsum('bhqk,bhkd->bhqd', diff, v)

    o = lax.map(block, (q_blocks, qpos_blocks))
    o = jnp.moveaxis(o, 0, 2).reshape(B, A_HEADS, S, 2 * A_HEAD_DIM)
    o = rms_norm(o, SUBLN_EPS) * subln_g * (1.0 - lambda_init)
    return o.transpose(0, 2, 1, 3).reshape(B, S, A_WIDTH) @ w_o


def shared_kv(x, silu_c, kv_ada_w, kv_ada_b, kv_w):
    B, S, _ = x.shape
    shift, scale = jnp.split((silu_c @ kv_ada_w + kv_ada_b)[:, None, :], 2, axis=-1)
    k, v = jnp.split(modulate(x, shift, scale) @ kv_w, 2, axis=-1)
    nb = -(-S // MOBA_BLOCK)
    pad = nb * MOBA_BLOCK - S

    def to_blocks(t):
        t = t.reshape(B, S, B_HEADS, B_HEAD_DIM).transpose(0, 2, 1, 3)
        t = jnp.pad(t, ((0, 0), (0, 0), (0, pad), (0, 0)))
        return t.reshape(B, B_HEADS, nb, MOBA_BLOCK, B_HEAD_DIM)

    k_blocks = to_blocks(k)
    v_blocks = to_blocks(v)
    k_means = jnp.mean(k_blocks.astype(jnp.float32), axis=3).astype(k_blocks.dtype)
    return k_blocks, v_blocks, k_means


def moba_attention(h, w_q, w_o, k_blocks, v_blocks, k_means, rel_bias):
    B, S, _ = h.shape
    nb = k_blocks.shape[2]
    q = (h @ w_q).reshape(B, S, B_HEADS, B_HEAD_DIM).transpose(0, 2, 1, 3)
    pos = jnp.arange(S)
    own = pos // MOBA_BLOCK
    gate = jnp.einsum('bhsd,bhnd->bhsn', q, k_means).astype(jnp.float32)
    fully_past = jnp.arange(nb)[None, :] < own[:, None]
    gate = jnp.where(fully_past, gate, -jnp.inf)
    n_sel = min(MOBA_TOPK, nb)
    _, sel = lax.top_k(gate, n_sel)
    blocks = jnp.concatenate(
        [sel, jnp.broadcast_to(own[:, None], (B, B_HEADS, S, 1)).astype(sel.dtype)], axis=-1)
    slot_ok = jnp.concatenate(
        [sel < own[:, None], jnp.ones((B, B_HEADS, S, 1), dtype=bool)], axis=-1)
    nc = S // MOBA_Q_CHUNK

    def chunks(t):
        return jnp.moveaxis(t.reshape(B, B_HEADS, nc, MOBA_Q_CHUNK, *t.shape[3:]), 2, 0)

    rel_t = rel_bias.T.astype(jnp.float32)
    head_idx = jnp.arange(B_HEADS)[None, :, None, None, None]
    offs = jnp.arange(MOBA_BLOCK)
    gather = jax.vmap(jax.vmap(lambda kb, ix: kb[ix]))
    scale = B_HEAD_DIM ** -0.5

    def chunk(args):
        qc, bc, okc, qpos = args
        kg = gather(k_blocks, bc)
        vg = gather(v_blocks, bc)
        dist = qpos[None, None, :, None, None] - (bc[..., None] * MOBA_BLOCK + offs)
        allowed = okc[..., None] & (dist >= 0)
        logits = (jnp.einsum('bhqd,bhqnkd->bhqnk', qc, kg).astype(jnp.float32) * scale
                  + rel_t[head_idx, t5_bucket(dist)])
        logits = jnp.where(allowed, logits, -jnp.inf)
        ns = bc.shape[-1]
        p = jax.nn.softmax(logits.reshape(B, B_HEADS, MOBA_Q_CHUNK, ns * MOBA_BLOCK), axis=-1)
        p = p.reshape(logits.shape).astype(vg.dtype)
        return jnp.einsum('bhqnk,bhqnkd->bhqd', p, vg)

    o = lax.map(chunk, (chunks(q), chunks(blocks), chunks(slot_ok), pos.reshape(nc, MOBA_Q_CHUNK)))
    o = jnp.moveaxis(o, 0, 2).reshape(B, B_HEADS, S, B_HEAD_DIM)
    return o.transpose(0, 2, 1, 3).reshape(B, S, B_WIDTH) @ w_o


def swiglu(t, wg, wu, wd):
    return (jax.nn.silu(t @ wg) * (t @ wu)) @ wd


def moe_ffn(h, router_w, router_b, w_gate, w_up, w_down, ws_gate, ws_up, ws_down):
    B, S, D = h.shape
    T = B * S
    t = h.reshape(T, D)
    affinity = jax.nn.sigmoid((t @ router_w).astype(jnp.float32))
    _, idx = lax.top_k(affinity + router_b.astype(jnp.float32), MOE_TOP_K)
    sel = jnp.take_along_axis(affinity, idx, axis=-1)
    wts = sel / jnp.sum(sel, axis=-1, keepdims=True) * ROUTED_SCALE
    gates = jnp.zeros_like(affinity).at[jnp.arange(T)[:, None], idx].set(wts).astype(t.dtype)
    C = math.gcd(T, MOE_TOKEN_CHUNK_MAX)
    n = T // C

    def chunk(args):
        tc, gc = args
        hid = (jax.nn.silu(jnp.einsum('cd,edf->cef', tc, w_gate))
               * jnp.einsum('cd,edf->cef', tc, w_up))
        return jnp.einsum('cef,efd->cd', hid * gc[:, :, None], w_down)

    routed = lax.map(chunk, (t.reshape(n, C, D), gates.reshape(n, C, N_EXPERTS))).reshape(T, D)
    return (routed + swiglu(t, ws_gate, ws_up, ws_down)).reshape(B, S, D)


def setup_inputs(seed: int = 0) -> dict:
    key = jax.random.key(seed)
    ks = jax.random.split(key, 24)
    D = D_MODEL

    def nrm(k, shape, scale):
        return jax.random.normal(k, shape, jnp.float32) * scale

    return {
        'x': nrm(ks[0], (BATCH, SEQ, D), 1.0),
        'c': nrm(ks[1], (BATCH, D), 1.0),
        'rel_bias': nrm(ks[2], (REL_BUCKETS, N_BIAS_COLS), 0.5),
        'ada_w': nrm(ks[3], (DEPTH, D, 6 * D), ADA_SCALE * D ** -0.5),
        'ada_b': nrm(ks[4], (DEPTH, 6 * D), 0.02),
        'a_w_qkv': nrm(ks[5], (N_A_LAYERS, D, 3 * A_WIDTH), D ** -0.5),
        'a_lambda': nrm(ks[6], (N_A_LAYERS, 4, A_HEAD_DIM), 0.1),
        'a_subln_g': 1.0 + nrm(ks[7], (N_A_LAYERS, 2 * A_HEAD_DIM), 0.02),
        'a_w_o': nrm(ks[8], (N_A_LAYERS, A_WIDTH, D), A_WIDTH ** -0.5),
        'kv_ada_w': nrm(ks[9], (D, 2 * D), ADA_SCALE * D ** -0.5),
        'kv_ada_b': nrm(ks[10], (2 * D,), 0.02),
        'kv_w': nrm(ks[11], (D, 2 * B_WIDTH), D ** -0.5),
        'b_w_q': nrm(ks[12], (N_B_LAYERS, D, B_WIDTH), D ** -0.5),
        'b_w_o': nrm(ks[13], (N_B_LAYERS, B_WIDTH, D), B_WIDTH ** -0.5),
        'router_w': nrm(ks[14], (DEPTH, D, N_EXPERTS), D ** -0.5),
        'router_b': nrm(ks[15], (DEPTH, N_EXPERTS), 0.01),
        'e_w_gate': nrm(ks[16], (DEPTH, N_EXPERTS, D, EXPERT_FF), D ** -0.5),
        'e_w_up': nrm(ks[17], (DEPTH, N_EXPERTS, D, EXPERT_FF), D ** -0.5),
        'e_w_down': nrm(ks[18], (DEPTH, N_EXPERTS, EXPERT_FF, D), EXPERT_FF ** -0.5),
        's_w_gate': nrm(ks[19], (DEPTH, D, SHARED_FF), D ** -0.5),
        's_w_up': nrm(ks[20], (DEPTH, D, SHARED_FF), D ** -0.5),
        's_w_down': nrm(ks[21], (DEPTH, SHARED_FF, D), SHARED_FF ** -0.5),
        'final_g': 1.0 + nrm(ks[22], (D,), 0.02),
    }


def reference(x, c, rel_bias, ada_w, ada_b, a_w_qkv, a_lambda, a_subln_g, a_w_o,
              kv_ada_w, kv_ada_b, kv_w, b_w_q, b_w_o, router_w, router_b,
              e_w_gate, e_w_up, e_w_down, s_w_gate, s_w_up, s_w_down, final_g):
    silu_c = jax.nn.silu(c)
    kv = None
    for layer in range(DEPTH):
        mod = (silu_c @ ada_w[layer] + ada_b[layer])[:, None, :]
        sh_mix, sc_mix, g_mix, sh_ffn, sc_ffn, g_ffn = jnp.split(mod, 6, axis=-1)
        h = modulate(x, sh_mix, sc_mix)
        if layer < N_A_LAYERS:
            lambda_init = 0.8 - 0.6 * math.exp(-0.3 * layer)
            mix = diff_attention(h, a_w_qkv[layer], a_lambda[layer], a_subln_g[layer],
                                 a_w_o[layer], rel_bias, lambda_init)
        else:
            if kv is None:
                kv = shared_kv(x, silu_c, kv_ada_w, kv_ada_b, kv_w)
            j = layer - N_A_LAYERS
            mix = moba_attention(h, b_w_q[j], b_w_o[j], kv[0], kv[1], kv[2], rel_bias)
        x = x + g_mix * mix
        h = modulate(x, sh_ffn, sc_ffn)
        x = x + g_ffn * moe_ffn(h, router_w[layer], router_b[layer], e_w_gate[layer],
                                e_w_up[layer], e_w_down[layer], s_w_gate[layer],
                                s_w_up[layer], s_w_down[layer])
    return rms_norm(x) * final_g
```

```python
import functools
import math

import numpy as np
import jax
import jax.numpy as jnp
from jax import lax
from jax.experimental import pallas as pl
from jax.experimental.pallas import tpu as pltpu

F32 = jnp.float32
BF16 = jnp.bfloat16

A_HEAD_DIM = 128
MOBA_BLOCK = 256
MOBA_TOPK = 3
REL_BUCKETS = 32
REL_MAX_DIST = 128
MOE_TOP_K = 8
ROUTED_SCALE = 2.5
NORM_EPS = 1e-6
SUBLN_EPS = 1e-5

LANES = 128
VMEM_LIMIT = 56 * 1024 * 1024


def _cparams(*sem):
    return pltpu.CompilerParams(dimension_semantics=sem, vmem_limit_bytes=VMEM_LIMIT)


def _tile(n, pref):
    t = min(pref, n)
    while n % t:
        t //= 2
    return t


def _dot(a, b):
    return jnp.dot(a, b, preferred_element_type=F32)


def _dot_t(a, b):
    return lax.dot_general(a, b, (((1,), (1,)), ((), ())), preferred_element_type=F32)


def _ada_kernel(c_ref, w_ref, b_ref, o_ref):
    c = c_ref[...]
    s = (c * jax.nn.sigmoid(c)).astype(BF16)
    o_ref[...] = _dot(s, w_ref[...].astype(BF16)) + b_ref[...]


def _ada_proj(c_pad, w, b):
    d, n = w.shape
    tn = _tile(n, 512)
    return pl.pallas_call(
        _ada_kernel,
        grid=(n // tn,),
        in_specs=[pl.BlockSpec((8, d), lambda j: (0, 0)),
                  pl.BlockSpec((d, tn), lambda j: (0, j)),
                  pl.BlockSpec((1, tn), lambda j: (0, j))],
        out_specs=pl.BlockSpec((8, tn), lambda j: (0, j)),
        out_shape=jax.ShapeDtypeStruct((8, n), F32),
        compiler_params=_cparams("parallel"),
        name="ada_proj",
    )(c_pad, w, b.reshape(1, n))


def _resmod_kernel(*refs, n_add, n_mod, emit_x, add_one):
    x_ref = refs[0]
    add_refs = refs[1:1 + n_add]
    pos = 1 + n_add
    x = x_ref[0]
    if n_add:
        g = refs[pos][0]
        pos += 1
        tot = add_refs[0][0].astype(F32)
        for r in add_refs[1:]:
            tot = tot + r[0].astype(F32)
        x = x + g * tot
    mod_refs = refs[pos:pos + 2 * n_mod]
    out_refs = refs[pos + 2 * n_mod:]
    oi = 0
    if emit_x:
        out_refs[0][0] = x
        oi = 1
    y = x * lax.rsqrt(jnp.mean(x * x, axis=-1, keepdims=True) + NORM_EPS)
    for k in range(n_mod):
        sh = mod_refs[2 * k][0]
        sc = mod_refs[2 * k + 1][0]
        if add_one[k]:
            sc = 1.0 + sc
        out_refs[oi + k][0] = (y * sc + sh).astype(out_refs[oi + k].dtype)


def _resmod(x, addends, gate, mods, emit_x, out_dtypes):
    b, s, d = x.shape
    ts = _tile(s, 128)
    row = pl.BlockSpec((1, ts, d), lambda i, j: (i, j, 0))

    def vec_spec(v):
        if v.shape[0] == 1:
            return pl.BlockSpec((1, 1, d), lambda i, j: (0, 0, 0))
        return pl.BlockSpec((1, 1, d), lambda i, j: (i, 0, 0))

    args = [x] + list(addends)
    specs = [row] * (1 + len(addends))
    if addends:
        args.append(gate)
        specs.append(vec_spec(gate))
    for sh, sc, _ in mods:
        args += [sh, sc]
        specs += [vec_spec(sh), vec_spec(sc)]
    out_shape, out_specs = [], []
    if emit_x:
        out_shape.append(jax.ShapeDtypeStruct((b, s, d), F32))
        out_specs.append(row)
    for dt in out_dtypes:
        out_shape.append(jax.ShapeDtypeStruct((b, s, d), dt))
        out_specs.append(row)
    kern = functools.partial(_resmod_kernel, n_add=len(addends), n_mod=len(mods), emit_x=emit_x,
                             add_one=tuple(m[2] for m in mods))
    return pl.pallas_call(
        kern, grid=(b, s // ts), in_specs=specs, out_specs=out_specs, out_shape=out_shape,
        compiler_params=_cparams("parallel", "parallel"), name="resmod",
    )(*args)


def _mm_kernel(a_ref, w_ref, o_ref):
    o_ref[...] = _dot(a_ref[...], w_ref[...]).astype(o_ref.dtype)


def _matmul(a, w, out_dtype):
    m, k = a.shape
    n = w.shape[1]
    tm, tn = _tile(m, 512), _tile(n, 1024)
    return pl.pallas_call(
        _mm_kernel,
        grid=(n // tn, m // tm),
        in_specs=[pl.BlockSpec((tm, k), lambda j, i: (i, 0)),
                  pl.BlockSpec((k, tn), lambda j, i: (0, j))],
        out_specs=pl.BlockSpec((tm, tn), lambda j, i: (i, j)),
        out_shape=jax.ShapeDtypeStruct((m, n), out_dtype),
        compiler_params=_cparams("parallel", "parallel"),
        name="proj",
    )(a, w)


def _t5_bucket_np(dist):
    n = np.maximum(dist, 0)
    max_exact = REL_BUCKETS // 2
    nf = np.maximum(n, 1).astype(np.float32)
    large = max_exact + (np.log(nf / np.float32(max_exact)) / np.float32(math.log(REL_MAX_DIST / max_exact))
                         * np.float32(REL_BUCKETS - max_exact)).astype(np.int32)
    large = np.minimum(large, REL_BUCKETS - 1)
    return np.where(n < max_exact, n, large).astype(np.int32)


def _bias_tile_kernel(rel_ref, bk_ref, o_ref):
    c = pl.program_id(0)
    bk = bk_ref[...]
    acc = jnp.zeros(bk.shape, F32)
    for bidx in range(REL_BUCKETS):
        acc = jnp.where(bk == bidx, rel_ref[bidx, c], acc)
    o_ref[0] = acc


def _bias_tiles(rel_bias, t, seq):
    i = np.arange(t)[:, None]
    j = np.arange(t)[None, :]
    bk = np.stack([_t5_bucket_np(i - j), _t5_bucket_np(i - j + t)])
    far = _t5_bucket_np(np.arange(t + 1, max(seq, t + 2)))
    assert (far == far[0]).all(), "distances beyond two tiles must share one bucket"
    ncol = rel_bias.shape[1]
    tiles = pl.pallas_call(
        _bias_tile_kernel,
        grid=(ncol,),
        in_specs=[pl.BlockSpec(memory_space=pltpu.SMEM),
                  pl.BlockSpec((2, t, t), lambda c: (0, 0, 0))],
        out_specs=pl.BlockSpec((1, 2, t, t), lambda c: (c, 0, 0, 0)),
        out_shape=jax.ShapeDtypeStruct((ncol, 2, t, t), F32),
        compiler_params=_cparams("parallel"),
        name="bias_tiles",
    )(rel_bias, jnp.asarray(bk))
    return tiles, int(far[0])


def _diff_attn_kernel(far_ref, lam_ref, g_ref, q_ref, k_ref, v_ref, bias_ref, o_ref,
                      m_sc, l_sc, acc_sc, *, tq, lambda_init):
    h = pl.program_id(1)
    qi = pl.program_id(2)
    dh = A_HEAD_DIM
    scale = dh ** -0.5
    q = q_ref[0]

    def tile(j, bias_of_map, causal, first):
        k = k_ref[0, pl.ds(pl.multiple_of(j * tq, tq), tq), :]
        v = v_ref[0, pl.ds(pl.multiple_of(j * tq, tq), tq), :]
        for mp in range(2):
            s = _dot_t(q[:, mp * dh:(mp + 1) * dh], k[:, mp * dh:(mp + 1) * dh]) * scale + bias_of_map(mp)
            if causal:
                r = lax.broadcasted_iota(jnp.int32, s.shape, 0)
                c = lax.broadcasted_iota(jnp.int32, s.shape, 1)
                s = jnp.where(r >= c, s, -jnp.inf)
            if first:
                m_new = jnp.max(s, axis=1, keepdims=True)
                p = jnp.exp(s - m_new)
                l_sc[mp] = jnp.sum(p, axis=1, keepdims=True)
                acc_sc[mp] = _dot(p.astype(BF16), v)
            else:
                m_prev = m_sc[mp]
                m_new = jnp.maximum(m_prev, jnp.max(s, axis=1, keepdims=True))
                alpha = jnp.exp(m_prev - m_new)
                p = jnp.exp(s - m_new)
                l_sc[mp] = alpha * l_sc[mp] + jnp.sum(p, axis=1, keepdims=True)
                acc_sc[mp] = alpha * acc_sc[mp] + _dot(p.astype(BF16), v)
            m_sc[mp] = m_new

    tile(qi, lambda mp: bias_ref[0, mp, 0], True, True)

    @pl.when(qi >= 1)
    def _():
        tile(qi - 1, lambda mp: bias_ref[0, mp, 1], False, False)

    def far_body(j, carry):
        tile(j, lambda mp: far_ref[2 * h + mp], False, False)
        return carry

    lax.fori_loop(0, jnp.maximum(qi - 1, 0), far_body, 0)

    lv = lam_ref[...]
    lam = (jnp.exp(jnp.sum(lv[0:1] * lv[1:2], axis=1, keepdims=True))
           - jnp.exp(jnp.sum(lv[2:3] * lv[3:4], axis=1, keepdims=True)) + lambda_init)
    o = acc_sc[0] / l_sc[0] - lam * (acc_sc[1] / l_sc[1])
    o = o * lax.rsqrt(jnp.mean(o * o, axis=-1, keepdims=True) + SUBLN_EPS)
    o_ref[0] = (o * g_ref[...] * (1.0 - lambda_init)).astype(o_ref.dtype)


def _diff_attention(qkv, far, lam_vecs, subln_g, bias_tiles, n_heads, lambda_init):
    b, s, w3 = qkv.shape
    w = w3 // 3
    hw = 2 * A_HEAD_DIM
    tq = bias_tiles.shape[-1]
    bt = bias_tiles.reshape(n_heads, 2, 2, tq, tq)
    kern = functools.partial(_diff_attn_kernel, tq=tq, lambda_init=lambda_init)
    return pl.pallas_call(
        kern,
        grid=(b, n_heads, s // tq),
        in_specs=[pl.BlockSpec(memory_space=pltpu.SMEM),
                  pl.BlockSpec((4, A_HEAD_DIM), lambda bi, h, qi: (0, 0)),
                  pl.BlockSpec((1, hw), lambda bi, h, qi: (0, 0)),
                  pl.BlockSpec((1, tq, hw), lambda bi, h, qi: (bi, qi, h)),
                  pl.BlockSpec((1, s, hw), lambda bi, h, qi: (bi, 0, n_heads + h)),
                  pl.BlockSpec((1, s, hw), lambda bi, h, qi: (bi, 0, 2 * n_heads + h)),
                  pl.BlockSpec((1, 2, 2, tq, tq), lambda bi, h, qi: (h, 0, 0, 0, 0))],
        out_specs=pl.BlockSpec((1, tq, hw), lambda bi, h, qi: (bi, qi, h)),
        out_shape=jax.ShapeDtypeStruct((b, s, w), BF16),
        scratch_shapes=[pltpu.VMEM((2, tq, 1), F32), pltpu.VMEM((2, tq, 1), F32),
                        pltpu.VMEM((2, tq, hw), F32)],
        compiler_params=_cparams("parallel", "parallel", "arbitrary"),
        name="diff_attn",
    )(far, lam_vecs, subln_g.reshape(1, hw), qkv, qkv, qkv, bt)


def _moba_kernel(far_ref, q_ref, k_ref, v_ref, bias_ref, o_ref,
                 kmean_sc, mask_sc, m_sc, l_sc, acc_sc, *, tq, nb):
    h = pl.program_id(1)
    qi = pl.program_id(2)
    dh = q_ref.shape[-1]
    seq = k_ref.shape[1]
    scale = dh ** -0.5
    q = q_ref[0]

    @pl.when(qi == 0)
    def _():
        r = lax.broadcasted_iota(jnp.int32, (LANES, seq), 0)
        c = lax.broadcasted_iota(jnp.int32, (LANES, seq), 1)
        ind = jnp.where((c >= r * tq) & (c < (r + 1) * tq), 1.0, 0.0).astype(BF16)
        kmean_sc[...] = _dot(ind, k_ref[0]) * (1.0 / tq)

    gate = _dot_t(q, kmean_sc[...].astype(BF16))
    lane = lax.broadcasted_iota(jnp.int32, gate.shape, 1)
    gate = jnp.where(lane < qi, gate, -jnp.inf)
    for n in range(nb):
        gn = gate[:, n:n + 1]
        ahead = (gate > gn) | ((gate == gn) & (lane < n))
        cnt = jnp.sum(jnp.where(ahead, 1.0, 0.0), axis=1, keepdims=True)
        mask_sc[n] = jnp.where(cnt < MOBA_TOPK, 0.0, -jnp.inf)

    def tile(j, bias, row_mask, causal, first):
        k = k_ref[0, pl.ds(pl.multiple_of(j * tq, tq), tq), :]
        v = v_ref[0, pl.ds(pl.multiple_of(j * tq, tq), tq), :]
        s = _dot_t(q, k) * scale + bias
        if causal:
            r = lax.broadcasted_iota(jnp.int32, s.shape, 0)
            c = lax.broadcasted_iota(jnp.int32, s.shape, 1)
            s = jnp.where(r >= c, s, -jnp.inf)
        if row_mask is not None:
            s = s + row_mask
        if first:
            m_new = jnp.max(s, axis=1, keepdims=True)
            p = jnp.exp(s - m_new)
            l_sc[...] = jnp.sum(p, axis=1, keepdims=True)
            acc_sc[...] = _dot(p.astype(BF16), v)
        else:
            m_prev = m_sc[...]
            m_new = jnp.maximum(m_prev, jnp.max(s, axis=1, keepdims=True))
            alpha = jnp.exp(m_prev - m_new)
            p = jnp.exp(s - m_new)
            l_sc[...] = alpha * l_sc[...] + jnp.sum(p, axis=1, keepdims=True)
            acc_sc[...] = alpha * acc_sc[...] + _dot(p.astype(BF16), v)
        m_sc[...] = m_new

    tile(qi, bias_ref[0, 0], None, True, True)

    @pl.when(qi >= 1)
    def _():
        tile(qi - 1, bias_ref[0, 1], mask_sc[qi - 1], False, False)

    def far_body(j, carry):
        tile(j, far_ref[h], mask_sc[j], False, False)
        return carry

    lax.fori_loop(0, jnp.maximum(qi - 1, 0), far_body, 0)
    o_ref[0] = (acc_sc[...] / l_sc[...]).astype(o_ref.dtype)


def _moba_attention(q, kv, far, bias_tiles, n_heads):
    b, s, w = q.shape
    dh = w // n_heads
    tq = MOBA_BLOCK
    nb = s // tq
    assert s % tq == 0 and bias_tiles.shape[-1] == tq and nb <= LANES
    kern = functools.partial(_moba_kernel, tq=tq, nb=nb)
    return pl.pallas_call(
        kern,
        grid=(b, n_heads, nb),
        in_specs=[pl.BlockSpec(memory_space=pltpu.SMEM),
                  pl.BlockSpec((1, tq, dh), lambda bi, h, qi: (bi, qi, h)),
                  pl.BlockSpec((1, s, dh), lambda bi, h, qi: (bi, 0, h)),
                  pl.BlockSpec((1, s, dh), lambda bi, h, qi: (bi, 0, n_heads + h)),
                  pl.BlockSpec((1, 2, tq, tq), lambda bi, h, qi: (h, 0, 0, 0))],
        out_specs=pl.BlockSpec((1, tq, dh), lambda bi, h, qi: (bi, qi, h)),
        out_shape=jax.ShapeDtypeStruct((b, s, w), BF16),
        scratch_shapes=[pltpu.VMEM((LANES, dh), F32), pltpu.VMEM((nb, tq, 1), F32),
                        pltpu.VMEM((tq, 1), F32), pltpu.VMEM((tq, 1), F32), pltpu.VMEM((tq, dh), F32)],
        compiler_params=_cparams("parallel", "parallel", "arbitrary"),
        name="moba_attn",
    )(far, q, kv, kv, bias_tiles)


def _router_kernel(h_ref, w_ref, b_ref, o_ref):
    aff = jax.nn.sigmoid(_dot(h_ref[...], w_ref[...]))
    work = aff + b_ref[...]
    lane = lax.broadcasted_iota(jnp.int32, aff.shape, 1).astype(F32)
    sel = jnp.zeros(aff.shape, F32)
    for _ in range(MOE_TOP_K):
        mx = jnp.max(work, axis=1, keepdims=True)
        first = jnp.min(jnp.where(work == mx, lane, float(LANES)), axis=1, keepdims=True)
        pick = lane == first
        sel = jnp.where(pick, aff, sel)
        work = jnp.where(pick, -jnp.inf, work)
    o_ref[...] = sel / jnp.sum(sel, axis=1, keepdims=True) * ROUTED_SCALE


def _router(h2d, router_w, router_b):
    t, d = h2d.shape
    e = router_w.shape[1]
    assert e <= LANES
    w = jnp.pad(router_w, ((0, 0), (0, LANES - e))).astype(BF16)
    bias = jnp.pad(router_b, (0, LANES - e), constant_values=-jnp.inf).reshape(1, LANES)
    tm = _tile(t, 256)
    return pl.pallas_call(
        _router_kernel,
        grid=(t // tm,),
        in_specs=[pl.BlockSpec((tm, d), lambda i: (i, 0)),
                  pl.BlockSpec((d, LANES), lambda i: (0, 0)),
                  pl.BlockSpec((1, LANES), lambda i: (0, 0))],
        out_specs=pl.BlockSpec((tm, LANES), lambda i: (i, 0)),
        out_shape=jax.ShapeDtypeStruct((t, LANES), F32),
        compiler_params=_cparams("parallel"),
        name="router",
    )(h2d, w, bias)


def _experts_kernel(x_ref, g_ref, wg_ref, wu_ref, wd_ref, o_ref):
    e = pl.program_id(1)
    x = x_ref[...]
    hg = _dot(x, wg_ref[0])
    hu = _dot(x, wu_ref[0])
    lane = lax.broadcasted_iota(jnp.int32, g_ref.shape, 1)
    ge = jnp.sum(jnp.where(lane == e, g_ref[...], 0.0), axis=1, keepdims=True)
    hid = (hg * jax.nn.sigmoid(hg) * hu * ge).astype(BF16)
    y = _dot(hid, wd_ref[0])

    @pl.when(e == 0)
    def _():
        o_ref[...] = y

    @pl.when(e > 0)
    def _():
        o_ref[...] += y


def _experts(h2d, gates, wg, wu, wd):
    t, d = h2d.shape
    ne, _, f = wg.shape
    tm = _tile(t, 512)
    return pl.pallas_call(
        _experts_kernel,
        grid=(t // tm, ne),
        in_specs=[pl.BlockSpec((tm, d), lambda i, e: (i, 0)),
                  pl.BlockSpec((tm, LANES), lambda i, e: (i, 0)),
                  pl.BlockSpec((1, d, f), lambda i, e: (e, 0, 0)),
                  pl.BlockSpec((1, d, f), lambda i, e: (e, 0, 0)),
                  pl.BlockSpec((1, f, d), lambda i, e: (e, 0, 0))],
        out_specs=pl.BlockSpec((tm, d), lambda i, e: (i, 0)),
        out_shape=jax.ShapeDtypeStruct((t, d), F32),
        compiler_params=_cparams("parallel", "arbitrary"),
        name="experts",
    )(h2d, gates, wg, wu, wd)


def _moe(h, router_w, router_b, wg, wu, wd, sg, su, sd):
    b, s, d = h.shape
    h2d = h.reshape(b * s, d)
    gates = _router(h2d, router_w, router_b)
    routed = _experts(h2d, gates, wg.astype(BF16), wu.astype(BF16), wd.astype(BF16))
    ones = jnp.ones((b * s, LANES), F32)
    shared = _experts(h2d, ones, sg.astype(BF16)[None], su.astype(BF16)[None], sd.astype(BF16)[None])
    return routed.reshape(b, s, d), shared.reshape(b, s, d)


def kernel(x, c, rel_bias, ada_w, ada_b, a_w_qkv, a_lambda, a_subln_g, a_w_o, kv_ada_w, kv_ada_b, kv_w,
           b_w_q, b_w_o, router_w, router_b, e_w_gate, e_w_up, e_w_down, s_w_gate, s_w_up, s_w_down, final_g):
    b, s, d = x.shape
    depth = ada_w.shape[0]
    n_a = a_w_qkv.shape[0]
    a_heads = a_w_qkv.shape[2] // (3 * 2 * A_HEAD_DIM)
    b_heads = rel_bias.shape[1]
    assert 2 * a_heads == b_heads

    c_pad = jnp.pad(c, ((0, 8 - b), (0, 0)))

    def ada(w, bias, parts):
        return _ada_proj(c_pad, w, bias)[:b].reshape(b, parts, 1, d)

    bias_tiles, far_bucket = _bias_tiles(rel_bias, MOBA_BLOCK, s)
    far = rel_bias[far_bucket]

    def vec(m, k):
        return m[:, k]

    mod = ada(ada_w[0], ada_b[0], 6)
    (h,) = _resmod(x, [], None, [(vec(mod, 0), vec(mod, 1), True)], False, [BF16])
    kv = None
    for layer in range(depth):
        if layer < n_a:
            lambda_init = 0.8 - 0.6 * math.exp(-0.3 * layer)
            qkv = _matmul(h.reshape(b * s, d), a_w_qkv[layer].astype(BF16), BF16)
            o = _diff_attention(qkv.reshape(b, s, -1), far, a_lambda[layer], a_subln_g[layer], bias_tiles,
                                a_heads, lambda_init)
            w_o = a_w_o[layer]
        else:
            j = layer - n_a
            hq = h
            if kv is None:
                hq, hkv = h
                kv = _matmul(hkv.reshape(b * s, d), kv_w.astype(BF16), BF16).reshape(b, s, -1)
            q = _matmul(hq.reshape(b * s, d), b_w_q[j].astype(BF16), BF16).reshape(b, s, -1)
            o = _moba_attention(q, kv, far, bias_tiles, b_heads)
            w_o = b_w_o[j]
        mix = _matmul(o.reshape(b * s, -1), w_o.astype(BF16), F32).reshape(b, s, d)
        x, h = _resmod(x, [mix], vec(mod, 2), [(vec(mod, 3), vec(mod, 4), True)], True, [BF16])
        routed, shared = _moe(h, router_w[layer], router_b[layer], e_w_gate[layer], e_w_up[layer],
                              e_w_down[layer], s_w_gate[layer], s_w_up[layer], s_w_down[layer])
        g_ffn = vec(mod, 5)
        if layer + 1 == depth:
            zero = jnp.zeros((1, 1, d), F32)
            (out,) = _resmod(x, [routed, shared], g_ffn, [(zero, final_g.reshape(1, 1, d), False)], False, [F32])
            return out
        mod = ada(ada_w[layer + 1], ada_b[layer + 1], 6)
        mods = [(vec(mod, 0), vec(mod, 1), True)]
        if layer + 1 == n_a:
            kvmod = ada(kv_ada_w, kv_ada_b, 2)
            mods.append((vec(kvmod, 0), vec(kvmod, 1), True))
            x, hq, hkv = _resmod(x, [routed, shared], g_ffn, mods, True, [BF16, BF16])
            h = (hq, hkv)
        else:
            x, h = _resmod(x, [routed, shared], g_ffn, mods, True, [BF16])
```

```python
import functools
import math

import numpy as np
import jax
import jax.numpy as jnp
from jax import lax
from jax.experimental import pallas as pl
from jax.experimental.pallas import tpu as pltpu

F32 = jnp.float32
BF16 = jnp.bfloat16

A_HEAD_DIM = 128
MOBA_BLOCK = 256
MOBA_TOPK = 3
REL_BUCKETS = 32
REL_MAX_DIST = 128
MOE_TOP_K = 8
ROUTED_SCALE = 2.5
NORM_EPS = 1e-6
SUBLN_EPS = 1e-5

LANES = 128
VMEM_LIMIT = 56 * 1024 * 1024
LOG2E = math.log2(math.e)


def _cparams(*sem):
    return pltpu.CompilerParams(dimension_semantics=sem, vmem_limit_bytes=VMEM_LIMIT)


def _tile(n, pref):
    t = min(pref, n)
    while n % t:
        t //= 2
    return t


def _dot(a, b):
    return jnp.dot(a, b, preferred_element_type=F32)


def _dot_t(a, b):
    return lax.dot_general(a, b, (((1,), (1,)), ((), ())), preferred_element_type=F32)


def _ada_kernel(c_ref, w_ref, b_ref, o_ref):
    c = c_ref[...]
    s = (c * jax.nn.sigmoid(c)).astype(BF16)
    o_ref[...] = _dot(s, w_ref[...].astype(BF16)) + b_ref[...]


def _ada_proj(c_pad, w, layer, b):
    _, d, n = w.shape
    tn = _tile(n, 512)
    return pl.pallas_call(
        _ada_kernel,
        grid=(n // tn,),
        in_specs=[pl.BlockSpec((8, d), lambda j: (0, 0)),
                  pl.BlockSpec((None, d, tn), lambda j: (layer, 0, j)),
                  pl.BlockSpec((1, tn), lambda j: (0, j))],
        out_specs=pl.BlockSpec((8, tn), lambda j: (0, j)),
        out_shape=jax.ShapeDtypeStruct((8, n), F32),
        compiler_params=_cparams("parallel"),
        name="ada_proj",
    )(c_pad, w, b.reshape(1, n))


def _resmod_kernel(*refs, n_add, n_mod, emit_x, add_one):
    x_ref = refs[0]
    add_refs = refs[1:1 + n_add]
    pos = 1 + n_add
    x = x_ref[0]
    if n_add:
        g = refs[pos][0]
        pos += 1
        tot = add_refs[0][0].astype(F32)
        for r in add_refs[1:]:
            tot = tot + r[0].astype(F32)
        x = x + g * tot
    mod_refs = refs[pos:pos + 2 * n_mod]
    out_refs = refs[pos + 2 * n_mod:]
    oi = 0
    if emit_x:
        out_refs[0][0] = x
        oi = 1
    y = x * lax.rsqrt(jnp.mean(x * x, axis=-1, keepdims=True) + NORM_EPS)
    for k in range(n_mod):
        sh = mod_refs[2 * k][0]
        sc = mod_refs[2 * k + 1][0]
        if add_one[k]:
            sc = 1.0 + sc
        out_refs[oi + k][0] = (y * sc + sh).astype(out_refs[oi + k].dtype)


def _resmod(x, addends, gate, mods, emit_x, out_dtypes):
    b, s, d = x.shape
    ts = _tile(s, 128)
    row = pl.BlockSpec((1, ts, d), lambda i, j: (i, j, 0))

    def vec_spec(v):
        if v.shape[0] == 1:
            return pl.BlockSpec((1, 1, d), lambda i, j: (0, 0, 0))
        return pl.BlockSpec((1, 1, d), lambda i, j: (i, 0, 0))

    args = [x] + list(addends)
    specs = [row] * (1 + len(addends))
    if addends:
        args.append(gate)
        specs.append(vec_spec(gate))
    for sh, sc, _ in mods:
        args += [sh, sc]
        specs += [vec_spec(sh), vec_spec(sc)]
    out_shape, out_specs = [], []
    if emit_x:
        out_shape.append(jax.ShapeDtypeStruct((b, s, d), F32))
        out_specs.append(row)
    for dt in out_dtypes:
        out_shape.append(jax.ShapeDtypeStruct((b, s, d), dt))
        out_specs.append(row)
    kern = functools.partial(_resmod_kernel, n_add=len(addends), n_mod=len(mods), emit_x=emit_x,
                             add_one=tuple(m[2] for m in mods))
    return pl.pallas_call(
        kern, grid=(b, s // ts), in_specs=specs, out_specs=out_specs, out_shape=out_shape,
        compiler_params=_cparams("parallel", "parallel"), name="resmod",
    )(*args)


def _mm_kernel(a_ref, w_ref, o_ref, wb_sc):
    @pl.when(pl.program_id(1) == 0)
    def _():
        wb_sc[...] = w_ref[...].astype(wb_sc.dtype)

    o_ref[...] = _dot(a_ref[...], wb_sc[...]).astype(o_ref.dtype)


def _matmul(a, w, layer, out_dtype):
    m, k = a.shape
    n = w.shape[2]
    tm, tn = _tile(m, 512), _tile(n, 512)
    return pl.pallas_call(
        _mm_kernel,
        grid=(n // tn, m // tm),
        in_specs=[pl.BlockSpec((tm, k), lambda j, i: (i, 0)),
                  pl.BlockSpec((None, k, tn), lambda j, i: (layer, 0, j))],
        out_specs=pl.BlockSpec((tm, tn), lambda j, i: (i, j)),
        out_shape=jax.ShapeDtypeStruct((m, n), out_dtype),
        scratch_shapes=[pltpu.VMEM((k, tn), BF16)],
        compiler_params=_cparams("parallel", "arbitrary"),
        name="proj",
    )(a, w)


def _t5_bucket_np(dist):
    n = np.maximum(dist, 0)
    max_exact = REL_BUCKETS // 2
    nf = np.maximum(n, 1).astype(np.float32)
    large = max_exact + (np.log(nf / np.float32(max_exact)) / np.float32(math.log(REL_MAX_DIST / max_exact))
                         * np.float32(REL_BUCKETS - max_exact)).astype(np.int32)
    large = np.minimum(large, REL_BUCKETS - 1)
    return np.where(n < max_exact, n, large).astype(np.int32)


def _bias_tile_kernel(rel_ref, bk_ref, o_ref, *, far_bucket, inv_scale):
    c = pl.program_id(0)
    bk = bk_ref[...]
    acc = jnp.zeros(bk.shape, F32)
    for bidx in range(REL_BUCKETS):
        acc = jnp.where(bk == bidx, rel_ref[bidx, c], acc)
    o_ref[0] = (acc - rel_ref[far_bucket, c]) * inv_scale


def _bias_tiles(rel_bias, t, seq, scale):
    i = np.arange(t)[:, None]
    j = np.arange(t)[None, :]
    bk = np.stack([_t5_bucket_np(i - j), _t5_bucket_np(i - j + t)])
    far = _t5_bucket_np(np.arange(t + 1, max(seq, t + 2)))
    assert (far == far[0]).all(), "distances beyond two tiles must share one bucket"
    ncol = rel_bias.shape[1]
    kern = functools.partial(_bias_tile_kernel, far_bucket=int(far[0]), inv_scale=1.0 / scale)
    return pl.pallas_call(
        kern,
        grid=(ncol,),
        in_specs=[pl.BlockSpec(memory_space=pltpu.SMEM),
                  pl.BlockSpec((2, t, t), lambda c: (0, 0, 0))],
        out_specs=pl.BlockSpec((1, 2, t, t), lambda c: (c, 0, 0, 0)),
        out_shape=jax.ShapeDtypeStruct((ncol, 2, t, t), F32),
        compiler_params=_cparams("parallel"),
        name="bias_tiles",
    )(rel_bias, jnp.asarray(bk))


def _lane_fold(x, op, part):
    for c in range(x.shape[1] // LANES):
        piece = x[:, c * LANES:(c + 1) * LANES]
        part = piece if part is None else op(part, piece)
    return part


def _softmax_rows(s_sc, p_sc, n_cols, c2):
    tile = s_sc.shape[0]
    mpart = None
    for c0 in range(0, n_cols, tile):
        mpart = _lane_fold(s_sc[:, c0:c0 + tile], jnp.maximum, mpart)
    m = jnp.max(mpart, axis=1, keepdims=True)
    lpart = None
    for c0 in range(0, n_cols, tile):
        p = jnp.exp2((s_sc[:, c0:c0 + tile] - m) * c2)
        p_sc[:, c0:c0 + tile] = p.astype(p_sc.dtype)
        lpart = _lane_fold(p, jnp.add, lpart)
    return jnp.sum(lpart, axis=1, keepdims=True)


def _diff_attn_kernel(lam_ref, g_ref, q_ref, k_ref, v_ref, bias_ref, o_ref, s0, s1, p0, p1, *, t, lambda_init):
    seq = q_ref.shape[1]
    dh = A_HEAD_DIM
    c2 = dh ** -0.5 * LOG2E
    lv = lam_ref[...]
    lam = (jnp.exp(jnp.sum(lv[0:1] * lv[1:2], axis=1, keepdims=True))
           - jnp.exp(jnp.sum(lv[2:3] * lv[3:4], axis=1, keepdims=True)) + lambda_init)
    causal = (lax.broadcasted_iota(jnp.int32, (t, t), 0) >= lax.broadcasted_iota(jnp.int32, (t, t), 1))
    for qi in range(seq // t):
        r0 = qi * t
        n_cols = r0 + t
        heads = []
        for mp, (s_sc, p_sc) in enumerate(((s0, p0), (s1, p1))):
            q = q_ref[0, r0:r0 + t, mp * dh:(mp + 1) * dh]
            for j in range(qi + 1):
                s = _dot_t(q, k_ref[0, j * t:(j + 1) * t, mp * dh:(mp + 1) * dh])
                if j == qi:
                    s = jnp.where(causal, s + bias_ref[0, mp, 0], -jnp.inf)
                elif j == qi - 1:
                    s = s + bias_ref[0, mp, 1]
                s_sc[:, j * t:(j + 1) * t] = s
            l = _softmax_rows(s_sc, p_sc, n_cols, c2)
            heads.append(_dot(p_sc[:, :n_cols], v_ref[0, :n_cols, :]) / l)
        o = heads[0] - lam * heads[1]
        o = o * lax.rsqrt(jnp.mean(o * o, axis=-1, keepdims=True) + SUBLN_EPS)
        o_ref[0, r0:r0 + t, :] = (o * g_ref[...] * (1.0 - lambda_init)).astype(o_ref.dtype)


def _diff_attention(qkv, lam_vecs, subln_g, bias_tiles, n_heads, lambda_init):
    b, s, w3 = qkv.shape
    w = w3 // 3
    hw = 2 * A_HEAD_DIM
    t = bias_tiles.shape[-1]
    bt = bias_tiles.reshape(n_heads, 2, 2, t, t)
    kern = functools.partial(_diff_attn_kernel, t=t, lambda_init=lambda_init)
    return pl.pallas_call(
        kern,
        grid=(b, n_heads),
        in_specs=[pl.BlockSpec((4, A_HEAD_DIM), lambda bi, h: (0, 0)),
                  pl.BlockSpec((1, hw), lambda bi, h: (0, 0)),
                  pl.BlockSpec((1, s, hw), lambda bi, h: (bi, 0, h)),
                  pl.BlockSpec((1, s, hw), lambda bi, h: (bi, 0, n_heads + h)),
                  pl.BlockSpec((1, s, hw), lambda bi, h: (bi, 0, 2 * n_heads + h)),
                  pl.BlockSpec((1, 2, 2, t, t), lambda bi, h: (h, 0, 0, 0, 0))],
        out_specs=pl.BlockSpec((1, s, hw), lambda bi, h: (bi, 0, h)),
        out_shape=jax.ShapeDtypeStruct((b, s, w), BF16),
        scratch_shapes=[pltpu.VMEM((t, s), F32), pltpu.VMEM((t, s), F32),
                        pltpu.VMEM((t, s), BF16), pltpu.VMEM((t, s), BF16)],
        compiler_params=_cparams("parallel", "parallel"),
        name="diff_attn",
    )(lam_vecs, subln_g.reshape(1, hw), qkv, qkv, qkv, bt)


def _moba_kernel(q_ref, k_ref, v_ref, bias_ref, o_ref, s_sc, p_sc, mask_sc, *, t):
    seq = q_ref.shape[1]
    dh = q_ref.shape[2]
    nb = seq // t
    c2 = dh ** -0.5 * LOG2E
    causal = (lax.broadcasted_iota(jnp.int32, (t, t), 0) >= lax.broadcasted_iota(jnp.int32, (t, t), 1))
    r = lax.broadcasted_iota(jnp.int32, (LANES, seq), 0)
    c = lax.broadcasted_iota(jnp.int32, (LANES, seq), 1)
    ind = jnp.where(c >= r * t, jnp.where(c < (r + 1) * t, 1.0, 0.0), 0.0).astype(k_ref.dtype)
    kmean = (_dot(ind, k_ref[0]) * (1.0 / t)).astype(k_ref.dtype)
    lane = lax.broadcasted_iota(jnp.int32, (t, LANES), 1)
    for qi in range(nb):
        r0 = qi * t
        n_cols = r0 + t
        q = q_ref[0, r0:r0 + t, :]
        masked = qi > MOBA_TOPK
        if masked:
            gate = jnp.where(lane < qi, _dot_t(q, kmean), -jnp.inf)
            for n in range(qi):
                gn = gate[:, n:n + 1]
                ahead = jnp.where(gate > gn, 1.0, jnp.where(lane < n, jnp.where(gate == gn, 1.0, 0.0), 0.0))
                cnt = jnp.sum(ahead, axis=1, keepdims=True)
                mask_sc[n] = jnp.broadcast_to(jnp.where(cnt < MOBA_TOPK, 0.0, -jnp.inf), (t, LANES))
        for j in range(qi + 1):
            s = _dot_t(q, k_ref[0, j * t:(j + 1) * t, :])
            if j == qi:
                s = jnp.where(causal, s + bias_ref[0, 0], -jnp.inf)
            else:
                if j == qi - 1:
                    s = s + bias_ref[0, 1]
                if masked:
                    mk = mask_sc[j]
                    s = s + jnp.concatenate([mk] * (t // LANES), axis=1)
            s_sc[:, j * t:(j + 1) * t] = s
        l = _softmax_rows(s_sc, p_sc, n_cols, c2)
        o_ref[0, r0:r0 + t, :] = (_dot(p_sc[:, :n_cols], v_ref[0, :n_cols, :]) / l).astype(o_ref.dtype)


def _moba_attention(q, kv, bias_tiles, n_heads):
    b, s, w = q.shape
    dh = w // n_heads
    t = MOBA_BLOCK
    nb = s // t
    assert s % t == 0 and bias_tiles.shape[-1] == t and nb <= LANES
    kern = functools.partial(_moba_kernel, t=t)
    return pl.pallas_call(
        kern,
        grid=(b, n_heads),
        in_specs=[pl.BlockSpec((1, s, dh), lambda bi, h: (bi, 0, h)),
                  pl.BlockSpec((1, s, dh), lambda bi, h: (bi, 0, h)),
                  pl.BlockSpec((1, s, dh), lambda bi, h: (bi, 0, n_heads + h)),
                  pl.BlockSpec((1, 2, t, t), lambda bi, h: (h, 0, 0, 0))],
        out_specs=pl.BlockSpec((1, s, dh), lambda bi, h: (bi, 0, h)),
        out_shape=jax.ShapeDtypeStruct((b, s, w), BF16),
        scratch_shapes=[pltpu.VMEM((t, s), F32), pltpu.VMEM((t, s), BF16), pltpu.VMEM((nb, t, LANES), F32)],
        compiler_params=_cparams("parallel", "parallel"),
        name="moba_attn",
    )(q, kv, kv, bias_tiles)


def _router_kernel(h_ref, w_ref, b_ref, o_ref):
    aff = jax.nn.sigmoid(_dot(h_ref[...], w_ref[...]))
    work = aff + b_ref[...]
    lane = lax.broadcasted_iota(jnp.int32, aff.shape, 1).astype(F32)
    sel = jnp.zeros(aff.shape, F32)
    for _ in range(MOE_TOP_K):
        mx = jnp.max(work, axis=1, keepdims=True)
        first = jnp.min(jnp.where(work == mx, lane, float(LANES)), axis=1, keepdims=True)
        pick = lane == first
        sel = jnp.where(pick, aff, sel)
        work = jnp.where(pick, -jnp.inf, work)
    o_ref[...] = sel / jnp.sum(sel, axis=1, keepdims=True) * ROUTED_SCALE


def _router(h2d, router_w, router_b):
    t, d = h2d.shape
    e = router_w.shape[1]
    assert e <= LANES
    w = jnp.pad(router_w, ((0, 0), (0, LANES - e))).astype(BF16)
    bias = jnp.pad(router_b, (0, LANES - e), constant_values=-jnp.inf).reshape(1, LANES)
    tm = _tile(t, 256)
    return pl.pallas_call(
        _router_kernel,
        grid=(t // tm,),
        in_specs=[pl.BlockSpec((tm, d), lambda i: (i, 0)),
                  pl.BlockSpec((d, LANES), lambda i: (0, 0)),
                  pl.BlockSpec((1, LANES), lambda i: (0, 0))],
        out_specs=pl.BlockSpec((tm, LANES), lambda i: (i, 0)),
        out_shape=jax.ShapeDtypeStruct((t, LANES), F32),
        compiler_params=_cparams("parallel"),
        name="router",
    )(h2d, w, bias)


def _experts_kernel(x_ref, g_ref, wg_ref, wu_ref, wd_ref, o_ref):
    e = pl.program_id(1)
    x = x_ref[...]
    hg = _dot(x, wg_ref[0])
    hu = _dot(x, wu_ref[0])
    lane = lax.broadcasted_iota(jnp.int32, g_ref.shape, 1)
    ge = jnp.sum(jnp.where(lane == e, g_ref[...], 0.0), axis=1, keepdims=True)
    hid = (hg * jax.nn.sigmoid(hg) * hu * ge).astype(BF16)
    y = _dot(hid, wd_ref[0])

    @pl.when(e == 0)
    def _():
        o_ref[...] = y

    @pl.when(e > 0)
    def _():
        o_ref[...] += y


def _experts(h2d, gates, wg, wu, wd):
    t, d = h2d.shape
    ne, _, f = wg.shape
    tm = _tile(t, 512)
    return pl.pallas_call(
        _experts_kernel,
        grid=(t // tm, ne),
        in_specs=[pl.BlockSpec((tm, d), lambda i, e: (i, 0)),
                  pl.BlockSpec((tm, LANES), lambda i, e: (i, 0)),
                  pl.BlockSpec((1, d, f), lambda i, e: (e, 0, 0)),
                  pl.BlockSpec((1, d, f), lambda i, e: (e, 0, 0)),
                  pl.BlockSpec((1, f, d), lambda i, e: (e, 0, 0))],
        out_specs=pl.BlockSpec((tm, d), lambda i, e: (i, 0)),
        out_shape=jax.ShapeDtypeStruct((t, d), F32),
        compiler_params=_cparams("parallel", "arbitrary"),
        name="experts",
    )(h2d, gates, wg, wu, wd)


def _moe(h, router_w, router_b, wg, wu, wd, sg, su, sd):
    b, s, d = h.shape
    h2d = h.reshape(b * s, d)
    gates = _router(h2d, router_w, router_b)
    routed = _experts(h2d, gates, wg.astype(BF16), wu.astype(BF16), wd.astype(BF16))
    ones = jnp.ones((b * s, LANES), F32)
    shared = _experts(h2d, ones, sg.astype(BF16)[None], su.astype(BF16)[None], sd.astype(BF16)[None])
    return routed.reshape(b, s, d), shared.reshape(b, s, d)


def kernel(x, c, rel_bias, ada_w, ada_b, a_w_qkv, a_lambda, a_subln_g, a_w_o, kv_ada_w, kv_ada_b, kv_w,
           b_w_q, b_w_o, router_w, router_b, e_w_gate, e_w_up, e_w_down, s_w_gate, s_w_up, s_w_down, final_g):
    b, s, d = x.shape
    depth = ada_w.shape[0]
    n_a = a_w_qkv.shape[0]
    a_heads = a_w_qkv.shape[2] // (3 * 2 * A_HEAD_DIM)
    b_heads = rel_bias.shape[1]
    b_head_dim = b_w_q.shape[2] // b_heads
    assert 2 * a_heads == b_heads and 1 <= n_a < depth and b_head_dim == A_HEAD_DIM

    c_pad = jnp.pad(c, ((0, 8 - b), (0, 0)))

    def ada(w, layer, bias, parts):
        return _ada_proj(c_pad, w, layer, bias)[:b].reshape(b, parts, 1, d)

    bias_tiles = _bias_tiles(rel_bias, MOBA_BLOCK, s, A_HEAD_DIM ** -0.5)

    def vec(m, k):
        return m[:, k]

    mod = ada(ada_w, 0, ada_b[0], 6)
    (h,) = _resmod(x, [], None, [(vec(mod, 0), vec(mod, 1), True)], False, [BF16])
    kv = None
    for layer in range(depth):
        if layer < n_a:
            lambda_init = 0.8 - 0.6 * math.exp(-0.3 * layer)
            qkv = _matmul(h.reshape(b * s, d), a_w_qkv, layer, BF16)
            o = _diff_attention(qkv.reshape(b, s, -1), a_lambda[layer], a_subln_g[layer], bias_tiles,
                                a_heads, lambda_init)
            w_o, w_o_layer = a_w_o, layer
        else:
            j = layer - n_a
            hq = h
            if kv is None:
                hq, hkv = h
                kv = _matmul(hkv.reshape(b * s, d), kv_w[None], 0, BF16).reshape(b, s, -1)
            q = _matmul(hq.reshape(b * s, d), b_w_q, j, BF16).reshape(b, s, -1)
            o = _moba_attention(q, kv, bias_tiles, b_heads)
            w_o, w_o_layer = b_w_o, j
        mix = _matmul(o.reshape(b * s, -1), w_o, w_o_layer, F32).reshape(b, s, d)
        x, h = _resmod(x, [mix], vec(mod, 2), [(vec(mod, 3), vec(mod, 4), True)], True, [BF16])
        routed, shared = _moe(h, router_w[layer], router_b[layer], e_w_gate[layer], e_w_up[layer],
                              e_w_down[layer], s_w_gate[layer], s_w_up[layer], s_w_down[layer])
        g_ffn = vec(mod, 5)
        if layer + 1 == depth:
            zero = jnp.zeros((1, 1, d), F32)
            (out,) = _resmod(x, [routed, shared], g_ffn, [(zero, final_g.reshape(1, 1, d), False)], False, [F32])
            return out
        mod = ada(ada_w, layer + 1, ada_b[layer + 1], 6)
        mods = [(vec(mod, 0), vec(mod, 1), True)]
        if layer + 1 == n_a:
            kvmod = ada(kv_ada_w[None], 0, kv_ada_b, 2)
            mods.append((vec(kvmod, 0), vec(kvmod, 1), True))
            x, hq, hkv = _resmod(x, [routed, shared], g_ffn, mods, True, [BF16, BF16])
            h = (hq, hkv)
        else:
            x, h = _resmod(x, [routed, shared], g_ffn, mods, True, [BF16])
```

```python
import functools
import math

import numpy as np
import jax
import jax.numpy as jnp
from jax import lax
from jax.experimental import pallas as pl
from jax.experimental.pallas import tpu as pltpu

F32 = jnp.float32
BF16 = jnp.bfloat16

A_HEAD_DIM = 128
MOBA_BLOCK = 256
MOBA_TOPK = 3
REL_BUCKETS = 32
REL_MAX_DIST = 128
MOE_TOP_K = 8
ROUTED_SCALE = 2.5
NORM_EPS = 1e-6
SUBLN_EPS = 1e-5

LANES = 128
SUBLANES = 8
VMEM_LIMIT = 56 * 1024 * 1024
LOG2E = math.log2(math.e)


def _cparams(*sem):
    return pltpu.CompilerParams(dimension_semantics=sem, vmem_limit_bytes=VMEM_LIMIT)


def _tile(n, pref):
    t = min(pref, n)
    while n % t:
        t //= 2
    return t


def _dot(a, b):
    return jnp.dot(a, b, preferred_element_type=F32)


def _dot_t(a, b):
    return lax.dot_general(a, b, (((1,), (1,)), ((), ())), preferred_element_type=F32)


def _ada_kernel(c_ref, w_ref, b_ref, o_ref):
    c = c_ref[...]
    s = (c * jax.nn.sigmoid(c)).astype(BF16)
    o_ref[...] = _dot(s, w_ref[...].astype(BF16)) + b_ref[...]


def _ada_proj(c_pad, w, layer, b):
    _, d, n = w.shape
    tn = _tile(n, 512)
    return pl.pallas_call(
        _ada_kernel,
        grid=(n // tn,),
        in_specs=[pl.BlockSpec((8, d), lambda j: (0, 0)),
                  pl.BlockSpec((None, d, tn), lambda j: (layer, 0, j)),
                  pl.BlockSpec((1, tn), lambda j: (0, j))],
        out_specs=pl.BlockSpec((8, tn), lambda j: (0, j)),
        out_shape=jax.ShapeDtypeStruct((8, n), F32),
        compiler_params=_cparams("parallel"),
        name="ada_proj",
    )(c_pad, w, b.reshape(1, n))


def _resmod_kernel(*refs, n_add, n_mod, emit_x, add_one):
    x_ref = refs[0]
    add_refs = refs[1:1 + n_add]
    pos = 1 + n_add
    x = x_ref[0]
    if n_add:
        g = refs[pos][0]
        pos += 1
        tot = add_refs[0][0].astype(F32)
        for r in add_refs[1:]:
            tot = tot + r[0].astype(F32)
        x = x + g * tot
    mod_refs = refs[pos:pos + 2 * n_mod]
    out_refs = refs[pos + 2 * n_mod:]
    oi = 0
    if emit_x:
        out_refs[0][0] = x
        oi = 1
    y = x * lax.rsqrt(jnp.mean(x * x, axis=-1, keepdims=True) + NORM_EPS)
    for k in range(n_mod):
        sh = mod_refs[2 * k][0]
        sc = mod_refs[2 * k + 1][0]
        if add_one[k]:
            sc = 1.0 + sc
        out_refs[oi + k][0] = (y * sc + sh).astype(out_refs[oi + k].dtype)


def _resmod(x, addends, gate, mods, emit_x, out_dtypes):
    b, s, d = x.shape
    ts = _tile(s, 128)
    row = pl.BlockSpec((1, ts, d), lambda i, j: (i, j, 0))

    def vec_spec(v):
        if v.shape[0] == 1:
            return pl.BlockSpec((1, 1, d), lambda i, j: (0, 0, 0))
        return pl.BlockSpec((1, 1, d), lambda i, j: (i, 0, 0))

    args = [x] + list(addends)
    specs = [row] * (1 + len(addends))
    if addends:
        args.append(gate)
        specs.append(vec_spec(gate))
    for sh, sc, _ in mods:
        args += [sh, sc]
        specs += [vec_spec(sh), vec_spec(sc)]
    out_shape, out_specs = [], []
    if emit_x:
        out_shape.append(jax.ShapeDtypeStruct((b, s, d), F32))
        out_specs.append(row)
    for dt in out_dtypes:
        out_shape.append(jax.ShapeDtypeStruct((b, s, d), dt))
        out_specs.append(row)
    kern = functools.partial(_resmod_kernel, n_add=len(addends), n_mod=len(mods), emit_x=emit_x,
                             add_one=tuple(m[2] for m in mods))
    return pl.pallas_call(
        kern, grid=(b, s // ts), in_specs=specs, out_specs=out_specs, out_shape=out_shape,
        compiler_params=_cparams("parallel", "parallel"), name="resmod",
    )(*args)


def _mm_kernel(a_ref, w_ref, o_ref, wb_sc):
    @pl.when(pl.program_id(1) == 0)
    def _():
        wb_sc[...] = w_ref[...].astype(wb_sc.dtype)

    o_ref[...] = _dot(a_ref[...], wb_sc[...]).astype(o_ref.dtype)


def _matmul(a, w, layer, out_dtype):
    m, k = a.shape
    n = w.shape[2]
    tm, tn = _tile(m, 512), _tile(n, 512)
    return pl.pallas_call(
        _mm_kernel,
        grid=(n // tn, m // tm),
        in_specs=[pl.BlockSpec((tm, k), lambda j, i: (i, 0)),
                  pl.BlockSpec((None, k, tn), lambda j, i: (layer, 0, j))],
        out_specs=pl.BlockSpec((tm, tn), lambda j, i: (i, j)),
        out_shape=jax.ShapeDtypeStruct((m, n), out_dtype),
        scratch_shapes=[pltpu.VMEM((k, tn), BF16)],
        compiler_params=_cparams("parallel", "arbitrary"),
        name="proj",
    )(a, w)


def _t5_bucket_np(dist):
    n = np.maximum(dist, 0)
    max_exact = REL_BUCKETS // 2
    nf = np.maximum(n, 1).astype(np.float32)
    large = max_exact + (np.log(nf / np.float32(max_exact)) / np.float32(math.log(REL_MAX_DIST / max_exact))
                         * np.float32(REL_BUCKETS - max_exact)).astype(np.int32)
    large = np.minimum(large, REL_BUCKETS - 1)
    return np.where(n < max_exact, n, large).astype(np.int32)


def _bias_tile_kernel(rel_ref, bk_ref, o_ref, *, far_bucket, inv_scale):
    c = pl.program_id(0)
    bk = bk_ref[...]
    acc = jnp.zeros(bk.shape, F32)
    for bidx in range(REL_BUCKETS):
        acc = jnp.where(bk == bidx, rel_ref[bidx, c], acc)
    o_ref[0] = (acc - rel_ref[far_bucket, c]) * inv_scale


def _bias_tiles(rel_bias, t, seq, scale):
    i = np.arange(t)[:, None]
    j = np.arange(t)[None, :]
    bk = np.stack([_t5_bucket_np(i - j), _t5_bucket_np(i - j + t)])
    far = _t5_bucket_np(np.arange(t + 1, max(seq, t + 2)))
    assert (far == far[0]).all(), "distances beyond two tiles must share one bucket"
    ncol = rel_bias.shape[1]
    kern = functools.partial(_bias_tile_kernel, far_bucket=int(far[0]), inv_scale=1.0 / scale)
    return pl.pallas_call(
        kern,
        grid=(ncol,),
        in_specs=[pl.BlockSpec(memory_space=pltpu.SMEM),
                  pl.BlockSpec((2, t, t), lambda c: (0, 0, 0))],
        out_specs=pl.BlockSpec((1, 2, t, t), lambda c: (c, 0, 0, 0)),
        out_shape=jax.ShapeDtypeStruct((ncol, 2, t, t), F32),
        compiler_params=_cparams("parallel"),
        name="bias_tiles",
    )(rel_bias, jnp.asarray(bk))


def _lane_fold(x, op, part):
    for c in range(x.shape[1] // LANES):
        piece = x[:, c * LANES:(c + 1) * LANES]
        part = piece if part is None else op(part, piece)
    return part


def _softmax_rows(s_sc, p_sc, n_cols, c2):
    tile = s_sc.shape[0]
    mpart = None
    for c0 in range(0, n_cols, tile):
        mpart = _lane_fold(s_sc[:, c0:c0 + tile], jnp.maximum, mpart)
    m = jnp.max(mpart, axis=1, keepdims=True)
    lpart = None
    for c0 in range(0, n_cols, tile):
        p = jnp.exp2((s_sc[:, c0:c0 + tile] - m) * c2)
        p_sc[:, c0:c0 + tile] = p.astype(p_sc.dtype)
        lpart = _lane_fold(p, jnp.add, lpart)
    return jnp.sum(lpart, axis=1, keepdims=True)


def _diff_attn_kernel(lam_ref, g_ref, q_ref, k_ref, v_ref, bias_ref, o_ref, s0, s1, p0, p1, *, t, lambda_init):
    seq = q_ref.shape[1]
    dh = A_HEAD_DIM
    c2 = dh ** -0.5 * LOG2E
    lv = lam_ref[...]
    lam = (jnp.exp(jnp.sum(lv[0:1] * lv[1:2], axis=1, keepdims=True))
           - jnp.exp(jnp.sum(lv[2:3] * lv[3:4], axis=1, keepdims=True)) + lambda_init)
    causal = (lax.broadcasted_iota(jnp.int32, (t, t), 0) >= lax.broadcasted_iota(jnp.int32, (t, t), 1))
    for qi in range(seq // t):
        r0 = qi * t
        n_cols = r0 + t
        heads = []
        for mp, (s_sc, p_sc) in enumerate(((s0, p0), (s1, p1))):
            q = q_ref[0, r0:r0 + t, mp * dh:(mp + 1) * dh]
            for j in range(qi + 1):
                s = _dot_t(q, k_ref[0, j * t:(j + 1) * t, mp * dh:(mp + 1) * dh])
                if j == qi:
                    s = jnp.where(causal, s + bias_ref[0, mp, 0], -jnp.inf)
                elif j == qi - 1:
                    s = s + bias_ref[0, mp, 1]
                s_sc[:, j * t:(j + 1) * t] = s
            l = _softmax_rows(s_sc, p_sc, n_cols, c2)
            heads.append(_dot(p_sc[:, :n_cols], v_ref[0, :n_cols, :]) / l)
        o = heads[0] - lam * heads[1]
        o = o * lax.rsqrt(jnp.mean(o * o, axis=-1, keepdims=True) + SUBLN_EPS)
        o_ref[0, r0:r0 + t, :] = (o * g_ref[...] * (1.0 - lambda_init)).astype(o_ref.dtype)


def _diff_attention(qkv, lam_vecs, subln_g, bias_tiles, n_heads, lambda_init):
    b, s, w3 = qkv.shape
    w = w3 // 3
    hw = 2 * A_HEAD_DIM
    t = bias_tiles.shape[-1]
    bt = bias_tiles.reshape(n_heads, 2, 2, t, t)
    kern = functools.partial(_diff_attn_kernel, t=t, lambda_init=lambda_init)
    return pl.pallas_call(
        kern,
        grid=(b, n_heads),
        in_specs=[pl.BlockSpec((4, A_HEAD_DIM), lambda bi, h: (0, 0)),
                  pl.BlockSpec((1, hw), lambda bi, h: (0, 0)),
                  pl.BlockSpec((1, s, hw), lambda bi, h: (bi, 0, h)),
                  pl.BlockSpec((1, s, hw), lambda bi, h: (bi, 0, n_heads + h)),
                  pl.BlockSpec((1, s, hw), lambda bi, h: (bi, 0, 2 * n_heads + h)),
                  pl.BlockSpec((1, 2, 2, t, t), lambda bi, h: (h, 0, 0, 0, 0))],
        out_specs=pl.BlockSpec((1, s, hw), lambda bi, h: (bi, 0, h)),
        out_shape=jax.ShapeDtypeStruct((b, s, w), BF16),
        scratch_shapes=[pltpu.VMEM((t, s), F32), pltpu.VMEM((t, s), F32),
                        pltpu.VMEM((t, s), BF16), pltpu.VMEM((t, s), BF16)],
        compiler_params=_cparams("parallel", "parallel"),
        name="diff_attn",
    )(lam_vecs, subln_g.reshape(1, hw), qkv, qkv, qkv, bt)


def _moba_kernel(q_ref, k_ref, v_ref, bias_ref, o_ref, s_sc, p_sc, mask_sc, *, t):
    seq = q_ref.shape[1]
    dh = q_ref.shape[2]
    nb = seq // t
    c2 = dh ** -0.5 * LOG2E
    causal = (lax.broadcasted_iota(jnp.int32, (t, t), 0) >= lax.broadcasted_iota(jnp.int32, (t, t), 1))
    r = lax.broadcasted_iota(jnp.int32, (LANES, seq), 0)
    c = lax.broadcasted_iota(jnp.int32, (LANES, seq), 1)
    ind = jnp.where(c >= r * t, jnp.where(c < (r + 1) * t, 1.0, 0.0), 0.0).astype(k_ref.dtype)
    kmean = (_dot(ind, k_ref[0]) * (1.0 / t)).astype(k_ref.dtype)
    lane = lax.broadcasted_iota(jnp.int32, (t, LANES), 1)
    for qi in range(nb):
        r0 = qi * t
        n_cols = r0 + t
        q = q_ref[0, r0:r0 + t, :]
        masked = qi > MOBA_TOPK
        if masked:
            gate = jnp.where(lane < qi, _dot_t(q, kmean), -jnp.inf)
            for n in range(qi):
                gn = gate[:, n:n + 1]
                ahead = jnp.where(gate > gn, 1.0, jnp.where(lane < n, jnp.where(gate == gn, 1.0, 0.0), 0.0))
                cnt = jnp.sum(ahead, axis=1, keepdims=True)
                mask_sc[n] = jnp.broadcast_to(jnp.where(cnt < MOBA_TOPK, 0.0, -jnp.inf), (t, LANES))
        for j in range(qi + 1):
            s = _dot_t(q, k_ref[0, j * t:(j + 1) * t, :])
            if j == qi:
                s = jnp.where(causal, s + bias_ref[0, 0], -jnp.inf)
            else:
                if j == qi - 1:
                    s = s + bias_ref[0, 1]
                if masked:
                    mk = mask_sc[j]
                    s = s + jnp.concatenate([mk] * (t // LANES), axis=1)
            s_sc[:, j * t:(j + 1) * t] = s
        l = _softmax_rows(s_sc, p_sc, n_cols, c2)
        o_ref[0, r0:r0 + t, :] = (_dot(p_sc[:, :n_cols], v_ref[0, :n_cols, :]) / l).astype(o_ref.dtype)


def _moba_attention(q, kv, bias_tiles, n_heads):
    b, s, w = q.shape
    dh = w // n_heads
    t = MOBA_BLOCK
    nb = s // t
    assert s % t == 0 and bias_tiles.shape[-1] == t and nb <= LANES
    kern = functools.partial(_moba_kernel, t=t)
    return pl.pallas_call(
        kern,
        grid=(b, n_heads),
        in_specs=[pl.BlockSpec((1, s, dh), lambda bi, h: (bi, 0, h)),
                  pl.BlockSpec((1, s, dh), lambda bi, h: (bi, 0, h)),
                  pl.BlockSpec((1, s, dh), lambda bi, h: (bi, 0, n_heads + h)),
                  pl.BlockSpec((1, 2, t, t), lambda bi, h: (h, 0, 0, 0))],
        out_specs=pl.BlockSpec((1, s, dh), lambda bi, h: (bi, 0, h)),
        out_shape=jax.ShapeDtypeStruct((b, s, w), BF16),
        scratch_shapes=[pltpu.VMEM((t, s), F32), pltpu.VMEM((t, s), BF16), pltpu.VMEM((nb, t, LANES), F32)],
        compiler_params=_cparams("parallel", "parallel"),
        name="moba_attn",
    )(q, kv, kv, bias_tiles)


def _router_kernel(h_ref, w_ref, b_ref, e_ref, g_ref, r_ref, cnt_ref, carry_sc):
    @pl.when(pl.program_id(0) == 0)
    def _():
        carry_sc[...] = jnp.zeros_like(carry_sc)

    aff = jax.nn.sigmoid(_dot(h_ref[...], w_ref[...]))
    work = aff + b_ref[...]
    tm = aff.shape[0]
    lane = lax.broadcasted_iota(jnp.int32, aff.shape, 1).astype(F32)
    mask = jnp.zeros(aff.shape, F32)
    firsts, affs = [], []
    for _ in range(MOE_TOP_K):
        mx = jnp.max(work, axis=1, keepdims=True)
        first = jnp.min(jnp.where(work == mx, lane, float(LANES)), axis=1, keepdims=True)
        pick = lane == first
        affs.append(jnp.sum(jnp.where(pick, aff, 0.0), axis=1, keepdims=True))
        firsts.append(first)
        mask = jnp.where(pick, 1.0, mask)
        work = jnp.where(pick, -jnp.inf, work)
    denom = affs[0]
    for a in affs[1:]:
        denom = denom + a
    tri = jnp.where(lax.broadcasted_iota(jnp.int32, (tm, tm), 0) > lax.broadcasted_iota(jnp.int32, (tm, tm), 1),
                    1.0, 0.0).astype(BF16)
    carry = carry_sc[...]
    rank = _dot(tri, mask.astype(BF16)) + carry
    carry = carry + jnp.sum(mask, axis=0, keepdims=True)
    carry_sc[...] = carry
    cnt_ref[...] = jnp.broadcast_to(carry, cnt_ref.shape)
    e_out = jnp.zeros(aff.shape, F32)
    g_out = jnp.zeros(aff.shape, F32)
    r_out = jnp.zeros(aff.shape, F32)
    for k in range(MOE_TOP_K):
        slot = lane == float(k)
        e_out = jnp.where(slot, firsts[k], e_out)
        g_out = jnp.where(slot, affs[k] / denom * ROUTED_SCALE, g_out)
        r_out = jnp.where(slot, jnp.sum(jnp.where(lane == firsts[k], rank, 0.0), axis=1, keepdims=True), r_out)
    e_ref[...] = e_out.astype(jnp.int32)
    g_ref[...] = g_out
    r_ref[...] = r_out.astype(jnp.int32)


def _router(h2d, router_w, router_b):
    t, d = h2d.shape
    e = router_w.shape[1]
    assert e <= LANES and MOE_TOP_K <= LANES
    w = jnp.pad(router_w, ((0, 0), (0, LANES - e))).astype(BF16)
    bias = jnp.pad(router_b, (0, LANES - e), constant_values=-jnp.inf).reshape(1, LANES)
    tm = _tile(t, 256)
    tok = pl.BlockSpec((tm, LANES), lambda i: (i, 0))
    return pl.pallas_call(
        _router_kernel,
        grid=(t // tm,),
        in_specs=[pl.BlockSpec((tm, d), lambda i: (i, 0)),
                  pl.BlockSpec((d, LANES), lambda i: (0, 0)),
                  pl.BlockSpec((1, LANES), lambda i: (0, 0))],
        out_specs=[tok, tok, tok, pl.BlockSpec((8, LANES), lambda i: (0, 0))],
        out_shape=[jax.ShapeDtypeStruct((t, LANES), jnp.int32), jax.ShapeDtypeStruct((t, LANES), F32),
                   jax.ShapeDtypeStruct((t, LANES), jnp.int32), jax.ShapeDtypeStruct((8, LANES), F32)],
        scratch_shapes=[pltpu.VMEM((1, LANES), F32)],
        compiler_params=_cparams("arbitrary"),
        name="router",
    )(h2d, w, bias)


def _route_plan(e_idx, rank, cnt, n_exp, tm):
    t = e_idx.shape[0]
    counts = cnt[0, :n_exp].astype(jnp.int32)
    padded = (counts + tm - 1) // tm * tm
    ends = jnp.cumsum(padded)
    starts = ends - padded
    pos = (starts[e_idx[:, :MOE_TOP_K]] + rank[:, :MOE_TOP_K]).reshape(-1)
    n_tiles = t * MOE_TOP_K // tm + n_exp
    tile_expert = jnp.searchsorted(ends, jnp.arange(n_tiles, dtype=jnp.int32) * tm, side="right")
    tile_expert = jnp.minimum(tile_expert, n_exp - 1).astype(jnp.int32)
    n_used = (ends[-1:] // tm).astype(jnp.int32)
    pad_start = jnp.concatenate([starts + counts, ends[-1:]]).astype(jnp.int32)
    pad_len = jnp.concatenate([padded - counts, n_tiles * tm - ends[-1:]]).astype(jnp.int32)
    return pos, tile_expert, n_used, pad_start, pad_len


def _dispatch_kernel(pos_ref, pstart_ref, plen_ref, h_ref, xs_ref, zero_sc, sem, zsem, *, tt, n_exp, pad_sizes):
    i = pl.program_id(0)
    base = i * tt * MOE_TOP_K

    def row_copy(t, p):
        return pltpu.make_async_copy(h_ref.at[pl.ds(t, 1)], xs_ref.at[pl.ds(p, 1)], sem)

    def issue(t, c):
        for k in range(MOE_TOP_K):
            row_copy(t, pos_ref[base + t * MOE_TOP_K + k]).start()
        return c

    lax.fori_loop(0, tt, issue, 0)

    @pl.when(i == 0)
    def _():
        zero_sc[...] = jnp.zeros_like(zero_sc)

        def fill(e, c, wait):
            off = pstart_ref[e]
            n = plen_ref[e]

            def piece(off, bsz):
                cp = pltpu.make_async_copy(zero_sc.at[pl.ds(0, bsz)], xs_ref.at[pl.ds(off, bsz)], zsem)
                if wait:
                    cp.wait()
                else:
                    cp.start()

            head = n & (SUBLANES - 1)
            for r in range(SUBLANES - 1):
                pl.when(r < head)(functools.partial(piece, off + r, 1))
            off = pl.multiple_of(off + head, SUBLANES)
            for bsz in pad_sizes:
                pl.when((n & bsz) != 0)(functools.partial(piece, off, bsz))
                off = pl.multiple_of(off + (n & bsz), SUBLANES)
            return c

        big = pad_sizes[-1]

        def tail(j, c, wait):
            off = pl.multiple_of(pstart_ref[n_exp] + j * big, SUBLANES)
            cp = pltpu.make_async_copy(zero_sc, xs_ref.at[pl.ds(off, big)], zsem)
            if wait:
                cp.wait()
            else:
                cp.start()
            return c

        n_tail = plen_ref[n_exp] // big
        lax.fori_loop(0, n_exp, functools.partial(fill, wait=False), 0)
        lax.fori_loop(0, n_tail, functools.partial(tail, wait=False), 0)
        lax.fori_loop(0, n_exp, functools.partial(fill, wait=True), 0)
        lax.fori_loop(0, n_tail, functools.partial(tail, wait=True), 0)

    def drain(t, c):
        for k in range(MOE_TOP_K):
            row_copy(0, 0).wait()
        return c

    lax.fori_loop(0, tt, drain, 0)


def _dispatch(h2d, pos, pad_start, pad_len, n_rows, tm):
    t, d = h2d.shape
    tt = _tile(t, 128)
    n_exp = pad_start.shape[0] - 1
    assert tm % SUBLANES == 0
    pad_sizes = tuple(1 << p for p in range(SUBLANES.bit_length() - 1, (tm - 1).bit_length()))
    kern = functools.partial(_dispatch_kernel, tt=tt, n_exp=n_exp, pad_sizes=pad_sizes)
    return pl.pallas_call(
        kern,
        grid_spec=pltpu.PrefetchScalarGridSpec(
            num_scalar_prefetch=3,
            grid=(t // tt,),
            in_specs=[pl.BlockSpec((tt, d), lambda i, *_: (i, 0))],
            out_specs=pl.BlockSpec(memory_space=pl.ANY),
            scratch_shapes=[pltpu.VMEM((pad_sizes[-1], d), F32), pltpu.SemaphoreType.DMA(()),
                            pltpu.SemaphoreType.DMA(())]),
        out_shape=jax.ShapeDtypeStruct((n_rows, d), F32),
        compiler_params=_cparams("arbitrary"),
        name="moe_dispatch",
    )(pos, pad_start, pad_len, h2d)


def _experts_kernel(te_ref, nu_ref, x_ref, wg_ref, wu_ref, wd_ref, y_ref, wgu_sc, wd_sc):
    i = pl.program_id(0)
    f = wg_ref.shape[1]
    live = i < nu_ref[0]
    fresh = jnp.logical_or(i == 0, te_ref[i] != te_ref[jnp.maximum(i - 1, 0)])

    @pl.when(jnp.logical_and(live, fresh))
    def _():
        wgu_sc[:, :f] = wg_ref[...].astype(wgu_sc.dtype)
        wgu_sc[:, f:] = wu_ref[...].astype(wgu_sc.dtype)
        wd_sc[...] = wd_ref[...].astype(wd_sc.dtype)

    @pl.when(live)
    def _():
        hgu = _dot(x_ref[...].astype(wgu_sc.dtype), wgu_sc[...])
        hg = hgu[:, :f]
        hid = (hg * jax.nn.sigmoid(hg) * hgu[:, f:]).astype(wd_sc.dtype)
        y_ref[...] = _dot(hid, wd_sc[...])

    @pl.when(jnp.logical_not(live))
    def _():
        y_ref[...] = jnp.zeros_like(y_ref)


def _experts(xs, tile_expert, n_used, wg, wu, wd, layer, tm):
    n_rows, d = xs.shape
    f = wg.shape[3]
    n_tiles = n_rows // tm
    return pl.pallas_call(
        _experts_kernel,
        grid_spec=pltpu.PrefetchScalarGridSpec(
            num_scalar_prefetch=2,
            grid=(n_tiles,),
            in_specs=[pl.BlockSpec((tm, d), lambda i, te, nu: (jnp.minimum(i, nu[0] - 1), 0)),
                      pl.BlockSpec((None, None, d, f), lambda i, te, nu: (layer, te[i], 0, 0)),
                      pl.BlockSpec((None, None, d, f), lambda i, te, nu: (layer, te[i], 0, 0)),
                      pl.BlockSpec((None, None, f, d), lambda i, te, nu: (layer, te[i], 0, 0))],
            out_specs=pl.BlockSpec((tm, d), lambda i, te, nu: (i, 0)),
            scratch_shapes=[pltpu.VMEM((d, 2 * f), BF16), pltpu.VMEM((f, d), BF16)]),
        out_shape=jax.ShapeDtypeStruct((n_rows, d), F32),
        compiler_params=_cparams("arbitrary"),
        name="experts",
    )(tile_expert, n_used, xs, wg, wu, wd)


def _combine_kernel(pos_ref, g_ref, y_ref, o_ref, buf, sem, *, tt):
    i = pl.program_id(0)
    base = i * tt * MOE_TOP_K

    def row_copy(t, k, p):
        return pltpu.make_async_copy(y_ref.at[pl.ds(p, 1)], buf.at[k, pl.ds(t, 1)], sem)

    def issue(t, c):
        for k in range(MOE_TOP_K):
            row_copy(t, k, pos_ref[base + t * MOE_TOP_K + k]).start()
        return c

    def drain(t, c):
        for k in range(MOE_TOP_K):
            row_copy(0, 0, 0).wait()
        return c

    lax.fori_loop(0, tt, issue, 0)
    lax.fori_loop(0, tt, drain, 0)
    g = g_ref[...]
    acc = g[:, 0:1] * buf[0]
    for k in range(1, MOE_TOP_K):
        acc = acc + g[:, k:k + 1] * buf[k]
    o_ref[...] = acc


def _combine(y, pos, gates):
    t = gates.shape[0]
    d = y.shape[1]
    tt = _tile(t, 64)
    kern = functools.partial(_combine_kernel, tt=tt)
    return pl.pallas_call(
        kern,
        grid_spec=pltpu.PrefetchScalarGridSpec(
            num_scalar_prefetch=1,
            grid=(t // tt,),
            in_specs=[pl.BlockSpec((tt, LANES), lambda i, *_: (i, 0)),
                      pl.BlockSpec(memory_space=pl.ANY)],
            out_specs=pl.BlockSpec((tt, d), lambda i, *_: (i, 0)),
            scratch_shapes=[pltpu.VMEM((MOE_TOP_K, tt, d), F32), pltpu.SemaphoreType.DMA(())]),
        out_shape=jax.ShapeDtypeStruct((t, d), F32),
        compiler_params=_cparams("arbitrary"),
        name="moe_combine",
    )(pos, gates, y)


def _shared_kernel(x_ref, wg_ref, wu_ref, wd_ref, o_ref, wgu_sc, wd_sc):
    f = wg_ref.shape[1]

    @pl.when(pl.program_id(0) == 0)
    def _():
        wgu_sc[:, :f] = wg_ref[...].astype(wgu_sc.dtype)
        wgu_sc[:, f:] = wu_ref[...].astype(wgu_sc.dtype)
        wd_sc[...] = wd_ref[...].astype(wd_sc.dtype)

    hgu = _dot(x_ref[...], wgu_sc[...])
    hg = hgu[:, :f]
    hid = (hg * jax.nn.sigmoid(hg) * hgu[:, f:]).astype(wd_sc.dtype)
    o_ref[...] = _dot(hid, wd_sc[...])


def _shared_expert(h2d, wg, wu, wd, layer):
    t, d = h2d.shape
    f = wg.shape[2]
    tm = _tile(t, 512)
    return pl.pallas_call(
        _shared_kernel,
        grid=(t // tm,),
        in_specs=[pl.BlockSpec((tm, d), lambda i: (i, 0)),
                  pl.BlockSpec((None, d, f), lambda i: (layer, 0, 0)),
                  pl.BlockSpec((None, d, f), lambda i: (layer, 0, 0)),
                  pl.BlockSpec((None, f, d), lambda i: (layer, 0, 0))],
        out_specs=pl.BlockSpec((tm, d), lambda i: (i, 0)),
        out_shape=jax.ShapeDtypeStruct((t, d), F32),
        scratch_shapes=[pltpu.VMEM((d, 2 * f), BF16), pltpu.VMEM((f, d), BF16)],
        compiler_params=_cparams("arbitrary"),
        name="shared_expert",
    )(h2d, wg, wu, wd)


EXPERT_ROW_TILE = 256


def _moe(h, h32, layer, router_w, router_b, wg, wu, wd, sg, su, sd):
    b, s, d = h.shape
    t = b * s
    n_exp = wg.shape[1]
    tm = EXPERT_ROW_TILE
    h2d = h.reshape(t, d)
    e_idx, gates, rank, cnt = _router(h2d, router_w[layer], router_b[layer])
    pos, tile_expert, n_used, pad_start, pad_len = _route_plan(e_idx, rank, cnt, n_exp, tm)
    n_rows = t * MOE_TOP_K + n_exp * tm
    xs = _dispatch(h32.reshape(t, d), pos, pad_start, pad_len, n_rows, tm)
    y = _experts(xs, tile_expert, n_used, wg, wu, wd, layer, tm)
    routed = _combine(y, pos, gates)
    shared = _shared_expert(h2d, sg, su, sd, layer)
    return routed.reshape(b, s, d), shared.reshape(b, s, d)


def kernel(x, c, rel_bias, ada_w, ada_b, a_w_qkv, a_lambda, a_subln_g, a_w_o, kv_ada_w, kv_ada_b, kv_w,
           b_w_q, b_w_o, router_w, router_b, e_w_gate, e_w_up, e_w_down, s_w_gate, s_w_up, s_w_down, final_g):
    b, s, d = x.shape
    depth = ada_w.shape[0]
    n_a = a_w_qkv.shape[0]
    a_heads = a_w_qkv.shape[2] // (3 * 2 * A_HEAD_DIM)
    b_heads = rel_bias.shape[1]
    b_head_dim = b_w_q.shape[2] // b_heads
    assert 2 * a_heads == b_heads and 1 <= n_a < depth and b_head_dim == A_HEAD_DIM

    c_pad = jnp.pad(c, ((0, 8 - b), (0, 0)))

    def ada(w, layer, bias, parts):
        return _ada_proj(c_pad, w, layer, bias)[:b].reshape(b, parts, 1, d)

    bias_tiles = _bias_tiles(rel_bias, MOBA_BLOCK, s, A_HEAD_DIM ** -0.5)

    def vec(m, k):
        return m[:, k]

    mod = ada(ada_w, 0, ada_b[0], 6)
    (h,) = _resmod(x, [], None, [(vec(mod, 0), vec(mod, 1), True)], False, [BF16])
    kv = None
    for layer in range(depth):
        if layer < n_a:
            lambda_init = 0.8 - 0.6 * math.exp(-0.3 * layer)
            qkv = _matmul(h.reshape(b * s, d), a_w_qkv, layer, BF16)
            o = _diff_attention(qkv.reshape(b, s, -1), a_lambda[layer], a_subln_g[layer], bias_tiles,
                                a_heads, lambda_init)
            w_o, w_o_layer = a_w_o, layer
        else:
            j = layer - n_a
            hq = h
            if kv is None:
                hq, hkv = h
                kv = _matmul(hkv.reshape(b * s, d), kv_w[None], 0, BF16).reshape(b, s, -1)
            q = _matmul(hq.reshape(b * s, d), b_w_q, j, BF16).reshape(b, s, -1)
            o = _moba_attention(q, kv, bias_tiles, b_heads)
            w_o, w_o_layer = b_w_o, j
        mix = _matmul(o.reshape(b * s, -1), w_o, w_o_layer, F32).reshape(b, s, d)
        ffn_mod = (vec(mod, 3), vec(mod, 4), True)
        x, h, h32 = _resmod(x, [mix], vec(mod, 2), [ffn_mod, ffn_mod], True, [BF16, F32])
        routed, shared = _moe(h, h32, layer, router_w, router_b, e_w_gate, e_w_up, e_w_down,
                              s_w_gate, s_w_up, s_w_down)
        g_ffn = vec(mod, 5)
        if layer + 1 == depth:
            zero = jnp.zeros((1, 1, d), F32)
            (out,) = _resmod(x, [routed, shared], g_ffn, [(zero, final_g.reshape(1, 1, d), False)], False, [F32])
            return out
        mod = ada(ada_w, layer + 1, ada_b[layer + 1], 6)
        mods = [(vec(mod, 0), vec(mod, 1), True)]
        if layer + 1 == n_a:
            kvmod = ada(kv_ada_w[None], 0, kv_ada_b, 2)
            mods.append((vec(kvmod, 0), vec(kvmod, 1), True))
            x, hq, hkv = _resmod(x, [routed, shared], g_ffn, mods, True, [BF16, BF16])
            h = (hq, hkv)
        else:
            x, h = _resmod(x, [routed, shared], g_ffn, mods, True, [BF16])
```

```python
import functools
import math

import numpy as np
import jax
import jax.numpy as jnp
from jax import lax
from jax.experimental import pallas as pl
from jax.experimental.pallas import tpu as pltpu

F32 = jnp.float32
BF16 = jnp.bfloat16

A_HEAD_DIM = 128
MOBA_BLOCK = 256
MOBA_TOPK = 3
REL_BUCKETS = 32
REL_MAX_DIST = 128
MOE_TOP_K = 8
ROUTED_SCALE = 2.5
NORM_EPS = 1e-6
SUBLN_EPS = 1e-5

LANES = 128
SUBLANES = 8
VMEM_LIMIT = 56 * 1024 * 1024
LOG2E = math.log2(math.e)


def _cparams(*sem):
    return pltpu.CompilerParams(dimension_semantics=sem, vmem_limit_bytes=VMEM_LIMIT)


def _tile(n, pref):
    t = min(pref, n)
    while n % t:
        t //= 2
    return t


def _dot(a, b):
    return jnp.dot(a, b, preferred_element_type=F32)


def _dot_t(a, b):
    return lax.dot_general(a, b, (((1,), (1,)), ((), ())), preferred_element_type=F32)


def _ada_kernel(c_ref, w_ref, b_ref, o_ref):
    c = c_ref[...]
    s = (c * jax.nn.sigmoid(c)).astype(BF16)
    o_ref[...] = _dot(s, w_ref[...].astype(BF16)) + b_ref[...]


def _ada_proj(c_pad, w, layer, b):
    _, d, n = w.shape
    tn = _tile(n, 512)
    return pl.pallas_call(
        _ada_kernel,
        grid=(n // tn,),
        in_specs=[pl.BlockSpec((8, d), lambda j: (0, 0)),
                  pl.BlockSpec((None, d, tn), lambda j: (layer, 0, j)),
                  pl.BlockSpec((1, tn), lambda j: (0, j))],
        out_specs=pl.BlockSpec((8, tn), lambda j: (0, j)),
        out_shape=jax.ShapeDtypeStruct((8, n), F32),
        compiler_params=_cparams("parallel"),
        name="ada_proj",
    )(c_pad, w, b.reshape(1, n))


def _resmod_kernel(*refs, n_add, n_mod, emit_x, add_one):
    x_ref = refs[0]
    add_refs = refs[1:1 + n_add]
    pos = 1 + n_add
    x = x_ref[0]
    if n_add:
        g = refs[pos][0]
        pos += 1
        tot = add_refs[0][0].astype(F32)
        for r in add_refs[1:]:
            tot = tot + r[0].astype(F32)
        x = x + g * tot
    mod_refs = refs[pos:pos + 2 * n_mod]
    out_refs = refs[pos + 2 * n_mod:]
    oi = 0
    if emit_x:
        out_refs[0][0] = x
        oi = 1
    y = x * lax.rsqrt(jnp.mean(x * x, axis=-1, keepdims=True) + NORM_EPS)
    for k in range(n_mod):
        sh = mod_refs[2 * k][0]
        sc = mod_refs[2 * k + 1][0]
        if add_one[k]:
            sc = 1.0 + sc
        out_refs[oi + k][0] = (y * sc + sh).astype(out_refs[oi + k].dtype)


def _resmod(x, addends, gate, mods, emit_x, out_dtypes):
    b, s, d = x.shape
    ts = _tile(s, 128)
    row = pl.BlockSpec((1, ts, d), lambda i, j: (i, j, 0))

    def vec_spec(v):
        if v.shape[0] == 1:
            return pl.BlockSpec((1, 1, d), lambda i, j: (0, 0, 0))
        return pl.BlockSpec((1, 1, d), lambda i, j: (i, 0, 0))

    args = [x] + list(addends)
    specs = [row] * (1 + len(addends))
    if addends:
        args.append(gate)
        specs.append(vec_spec(gate))
    for sh, sc, _ in mods:
        args += [sh, sc]
        specs += [vec_spec(sh), vec_spec(sc)]
    out_shape, out_specs = [], []
    if emit_x:
        out_shape.append(jax.ShapeDtypeStruct((b, s, d), F32))
        out_specs.append(row)
    for dt in out_dtypes:
        out_shape.append(jax.ShapeDtypeStruct((b, s, d), dt))
        out_specs.append(row)
    kern = functools.partial(_resmod_kernel, n_add=len(addends), n_mod=len(mods), emit_x=emit_x,
                             add_one=tuple(m[2] for m in mods))
    return pl.pallas_call(
        kern, grid=(b, s // ts), in_specs=specs, out_specs=out_specs, out_shape=out_shape,
        compiler_params=_cparams("parallel", "parallel"), name="resmod",
    )(*args)


def _mm_kernel(a_ref, w_ref, o_ref, wb_sc):
    @pl.when(pl.program_id(1) == 0)
    def _():
        wb_sc[...] = w_ref[...].astype(wb_sc.dtype)

    o_ref[...] = _dot(a_ref[...], wb_sc[...]).astype(o_ref.dtype)


def _matmul(a, w, layer, out_dtype):
    m, k = a.shape
    n = w.shape[2]
    tm, tn = _tile(m, 512), _tile(n, 512)
    return pl.pallas_call(
        _mm_kernel,
        grid=(n // tn, m // tm),
        in_specs=[pl.BlockSpec((tm, k), lambda j, i: (i, 0)),
                  pl.BlockSpec((None, k, tn), lambda j, i: (layer, 0, j))],
        out_specs=pl.BlockSpec((tm, tn), lambda j, i: (i, j)),
        out_shape=jax.ShapeDtypeStruct((m, n), out_dtype),
        scratch_shapes=[pltpu.VMEM((k, tn), BF16)],
        compiler_params=_cparams("parallel", "arbitrary"),
        name="proj",
    )(a, w)


def _t5_bucket_np(dist):
    n = np.maximum(dist, 0)
    max_exact = REL_BUCKETS // 2
    nf = np.maximum(n, 1).astype(np.float32)
    large = max_exact + (np.log(nf / np.float32(max_exact)) / np.float32(math.log(REL_MAX_DIST / max_exact))
                         * np.float32(REL_BUCKETS - max_exact)).astype(np.int32)
    large = np.minimum(large, REL_BUCKETS - 1)
    return np.where(n < max_exact, n, large).astype(np.int32)


def _bias_tile_kernel(rel_ref, bk_ref, o_ref, *, far_bucket, inv_scale):
    c = pl.program_id(0)
    bk = bk_ref[...]
    acc = jnp.zeros(bk.shape, F32)
    for bidx in range(REL_BUCKETS):
        acc = jnp.where(bk == bidx, rel_ref[bidx, c], acc)
    o_ref[0] = (acc - rel_ref[far_bucket, c]) * inv_scale


def _bias_tiles(rel_bias, t, seq, scale):
    i = np.arange(t)[:, None]
    j = np.arange(t)[None, :]
    bk = np.stack([_t5_bucket_np(i - j), _t5_bucket_np(i - j + t)])
    far = _t5_bucket_np(np.arange(t + 1, max(seq, t + 2)))
    assert (far == far[0]).all(), "distances beyond two tiles must share one bucket"
    ncol = rel_bias.shape[1]
    kern = functools.partial(_bias_tile_kernel, far_bucket=int(far[0]), inv_scale=1.0 / scale)
    return pl.pallas_call(
        kern,
        grid=(ncol,),
        in_specs=[pl.BlockSpec(memory_space=pltpu.SMEM),
                  pl.BlockSpec((2, t, t), lambda c: (0, 0, 0))],
        out_specs=pl.BlockSpec((1, 2, t, t), lambda c: (c, 0, 0, 0)),
        out_shape=jax.ShapeDtypeStruct((ncol, 2, t, t), F32),
        compiler_params=_cparams("parallel"),
        name="bias_tiles",
    )(rel_bias, jnp.asarray(bk))


def _lane_fold(x, op, part):
    for c in range(x.shape[1] // LANES):
        piece = x[:, c * LANES:(c + 1) * LANES]
        part = piece if part is None else op(part, piece)
    return part


def _softmax_rows(s_sc, p_sc, n_cols, c2):
    tile = s_sc.shape[0]
    mpart = None
    for c0 in range(0, n_cols, tile):
        mpart = _lane_fold(s_sc[:, c0:c0 + tile], jnp.maximum, mpart)
    m = jnp.max(mpart, axis=1, keepdims=True)
    lpart = None
    for c0 in range(0, n_cols, tile):
        p = jnp.exp2((s_sc[:, c0:c0 + tile] - m) * c2)
        p_sc[:, c0:c0 + tile] = p.astype(p_sc.dtype)
        lpart = _lane_fold(p, jnp.add, lpart)
    return jnp.sum(lpart, axis=1, keepdims=True)


def _diff_attn_kernel(lam_ref, g_ref, q_ref, k_ref, v_ref, bias_ref, o_ref, s0, s1, p0, p1, *, t, lambda_init):
    seq = q_ref.shape[1]
    dh = A_HEAD_DIM
    c2 = dh ** -0.5 * LOG2E
    lv = lam_ref[...]
    lam = (jnp.exp(jnp.sum(lv[0:1] * lv[1:2], axis=1, keepdims=True))
           - jnp.exp(jnp.sum(lv[2:3] * lv[3:4], axis=1, keepdims=True)) + lambda_init)
    causal = (lax.broadcasted_iota(jnp.int32, (t, t), 0) >= lax.broadcasted_iota(jnp.int32, (t, t), 1))
    for qi in range(seq // t):
        r0 = qi * t
        n_cols = r0 + t
        heads = []
        for mp, (s_sc, p_sc) in enumerate(((s0, p0), (s1, p1))):
            q = q_ref[0, r0:r0 + t, mp * dh:(mp + 1) * dh]
            for j in range(qi + 1):
                s = _dot_t(q, k_ref[0, j * t:(j + 1) * t, mp * dh:(mp + 1) * dh])
                if j == qi:
                    s = jnp.where(causal, s + bias_ref[0, mp, 0], -jnp.inf)
                elif j == qi - 1:
                    s = s + bias_ref[0, mp, 1]
                s_sc[:, j * t:(j + 1) * t] = s
            l = _softmax_rows(s_sc, p_sc, n_cols, c2)
            heads.append(_dot(p_sc[:, :n_cols], v_ref[0, :n_cols, :]) / l)
        o = heads[0] - lam * heads[1]
        o = o * lax.rsqrt(jnp.mean(o * o, axis=-1, keepdims=True) + SUBLN_EPS)
        o_ref[0, r0:r0 + t, :] = (o * g_ref[...] * (1.0 - lambda_init)).astype(o_ref.dtype)


def _diff_attention(qkv, lam_vecs, subln_g, bias_tiles, n_heads, lambda_init):
    b, s, w3 = qkv.shape
    w = w3 // 3
    hw = 2 * A_HEAD_DIM
    t = bias_tiles.shape[-1]
    bt = bias_tiles.reshape(n_heads, 2, 2, t, t)
    kern = functools.partial(_diff_attn_kernel, t=t, lambda_init=lambda_init)
    return pl.pallas_call(
        kern,
        grid=(b, n_heads),
        in_specs=[pl.BlockSpec((4, A_HEAD_DIM), lambda bi, h: (0, 0)),
                  pl.BlockSpec((1, hw), lambda bi, h: (0, 0)),
                  pl.BlockSpec((1, s, hw), lambda bi, h: (bi, 0, h)),
                  pl.BlockSpec((1, s, hw), lambda bi, h: (bi, 0, n_heads + h)),
                  pl.BlockSpec((1, s, hw), lambda bi, h: (bi, 0, 2 * n_heads + h)),
                  pl.BlockSpec((1, 2, 2, t, t), lambda bi, h: (h, 0, 0, 0, 0))],
        out_specs=pl.BlockSpec((1, s, hw), lambda bi, h: (bi, 0, h)),
        out_shape=jax.ShapeDtypeStruct((b, s, w), BF16),
        scratch_shapes=[pltpu.VMEM((t, s), F32), pltpu.VMEM((t, s), F32),
                        pltpu.VMEM((t, s), BF16), pltpu.VMEM((t, s), BF16)],
        compiler_params=_cparams("parallel", "parallel"),
        name="diff_attn",
    )(lam_vecs, subln_g.reshape(1, hw), qkv, qkv, qkv, bt)


def _moba_kernel(q_ref, k_ref, v_ref, bias_ref, o_ref, s_sc, p_sc, mask_sc, *, t):
    seq = q_ref.shape[1]
    dh = q_ref.shape[2]
    nb = seq // t
    c2 = dh ** -0.5 * LOG2E
    causal = (lax.broadcasted_iota(jnp.int32, (t, t), 0) >= lax.broadcasted_iota(jnp.int32, (t, t), 1))
    r = lax.broadcasted_iota(jnp.int32, (LANES, seq), 0)
    c = lax.broadcasted_iota(jnp.int32, (LANES, seq), 1)
    ind = jnp.where(c >= r * t, jnp.where(c < (r + 1) * t, 1.0, 0.0), 0.0).astype(k_ref.dtype)
    kmean = (_dot(ind, k_ref[0]) * (1.0 / t)).astype(k_ref.dtype)
    lane = lax.broadcasted_iota(jnp.int32, (t, LANES), 1)
    for qi in range(nb):
        r0 = qi * t
        n_cols = r0 + t
        q = q_ref[0, r0:r0 + t, :]
        masked = qi > MOBA_TOPK
        if masked:
            gate = jnp.where(lane < qi, _dot_t(q, kmean), -jnp.inf)
            for n in range(qi):
                gn = gate[:, n:n + 1]
                ahead = jnp.where(gate > gn, 1.0, jnp.where(lane < n, jnp.where(gate == gn, 1.0, 0.0), 0.0))
                cnt = jnp.sum(ahead, axis=1, keepdims=True)
                mask_sc[n] = jnp.broadcast_to(jnp.where(cnt < MOBA_TOPK, 0.0, -jnp.inf), (t, LANES))
        for j in range(qi + 1):
            s = _dot_t(q, k_ref[0, j * t:(j + 1) * t, :])
            if j == qi:
                s = jnp.where(causal, s + bias_ref[0, 0], -jnp.inf)
            else:
                if j == qi - 1:
                    s = s + bias_ref[0, 1]
                if masked:
                    mk = mask_sc[j]
                    s = s + jnp.concatenate([mk] * (t // LANES), axis=1)
            s_sc[:, j * t:(j + 1) * t] = s
        l = _softmax_rows(s_sc, p_sc, n_cols, c2)
        o_ref[0, r0:r0 + t, :] = (_dot(p_sc[:, :n_cols], v_ref[0, :n_cols, :]) / l).astype(o_ref.dtype)


def _moba_attention(q, kv, bias_tiles, n_heads):
    b, s, w = q.shape
    dh = w // n_heads
    t = MOBA_BLOCK
    nb = s // t
    assert s % t == 0 and bias_tiles.shape[-1] == t and nb <= LANES
    kern = functools.partial(_moba_kernel, t=t)
    return pl.pallas_call(
        kern,
        grid=(b, n_heads),
        in_specs=[pl.BlockSpec((1, s, dh), lambda bi, h: (bi, 0, h)),
                  pl.BlockSpec((1, s, dh), lambda bi, h: (bi, 0, h)),
                  pl.BlockSpec((1, s, dh), lambda bi, h: (bi, 0, n_heads + h)),
                  pl.BlockSpec((1, 2, t, t), lambda bi, h: (h, 0, 0, 0))],
        out_specs=pl.BlockSpec((1, s, dh), lambda bi, h: (bi, 0, h)),
        out_shape=jax.ShapeDtypeStruct((b, s, w), BF16),
        scratch_shapes=[pltpu.VMEM((t, s), F32), pltpu.VMEM((t, s), BF16), pltpu.VMEM((nb, t, LANES), F32)],
        compiler_params=_cparams("parallel", "parallel"),
        name="moba_attn",
    )(q, kv, kv, bias_tiles)


def _router_kernel(h_ref, w_ref, b_ref, e_ref, g_ref, r_ref, cnt_ref, carry_sc):
    @pl.when(pl.program_id(0) == 0)
    def _():
        carry_sc[...] = jnp.zeros_like(carry_sc)

    aff = jax.nn.sigmoid(_dot(h_ref[...], w_ref[...]))
    work = aff + b_ref[...]
    tm = aff.shape[0]
    lane = lax.broadcasted_iota(jnp.int32, aff.shape, 1).astype(F32)
    mask = jnp.zeros(aff.shape, F32)
    firsts, affs = [], []
    for _ in range(MOE_TOP_K):
        mx = jnp.max(work, axis=1, keepdims=True)
        first = jnp.min(jnp.where(work == mx, lane, float(LANES)), axis=1, keepdims=True)
        pick = lane == first
        affs.append(jnp.sum(jnp.where(pick, aff, 0.0), axis=1, keepdims=True))
        firsts.append(first)
        mask = jnp.where(pick, 1.0, mask)
        work = jnp.where(pick, -jnp.inf, work)
    denom = affs[0]
    for a in affs[1:]:
        denom = denom + a
    tri = jnp.where(lax.broadcasted_iota(jnp.int32, (tm, tm), 0) > lax.broadcasted_iota(jnp.int32, (tm, tm), 1),
                    1.0, 0.0).astype(BF16)
    carry = carry_sc[...]
    rank = _dot(tri, mask.astype(BF16)) + carry
    carry = carry + jnp.sum(mask, axis=0, keepdims=True)
    carry_sc[...] = carry
    cnt_ref[...] = jnp.broadcast_to(carry, cnt_ref.shape)
    e_out = jnp.zeros(aff.shape, F32)
    g_out = jnp.zeros(aff.shape, F32)
    r_out = jnp.zeros(aff.shape, F32)
    for k in range(MOE_TOP_K):
        slot = lane == float(k)
        e_out = jnp.where(slot, firsts[k], e_out)
        g_out = jnp.where(slot, affs[k] / denom * ROUTED_SCALE, g_out)
        r_out = jnp.where(slot, jnp.sum(jnp.where(lane == firsts[k], rank, 0.0), axis=1, keepdims=True), r_out)
    e_ref[...] = e_out.astype(jnp.int32)
    g_ref[...] = g_out
    r_ref[...] = r_out.astype(jnp.int32)


def _router(h2d, router_w, router_b):
    t, d = h2d.shape
    e = router_w.shape[1]
    assert e <= LANES and MOE_TOP_K <= LANES
    w = jnp.pad(router_w, ((0, 0), (0, LANES - e))).astype(BF16)
    bias = jnp.pad(router_b, (0, LANES - e), constant_values=-jnp.inf).reshape(1, LANES)
    tm = _tile(t, 256)
    tok = pl.BlockSpec((tm, LANES), lambda i: (i, 0))
    return pl.pallas_call(
        _router_kernel,
        grid=(t // tm,),
        in_specs=[pl.BlockSpec((tm, d), lambda i: (i, 0)),
                  pl.BlockSpec((d, LANES), lambda i: (0, 0)),
                  pl.BlockSpec((1, LANES), lambda i: (0, 0))],
        out_specs=[tok, tok, tok, pl.BlockSpec((8, LANES), lambda i: (0, 0))],
        out_shape=[jax.ShapeDtypeStruct((t, LANES), jnp.int32), jax.ShapeDtypeStruct((t, LANES), F32),
                   jax.ShapeDtypeStruct((t, LANES), jnp.int32), jax.ShapeDtypeStruct((8, LANES), F32)],
        scratch_shapes=[pltpu.VMEM((1, LANES), F32)],
        compiler_params=_cparams("arbitrary"),
        name="router",
    )(h2d, w, bias)


def _route_plan(e_idx, rank, cnt, n_exp, tm):
    t = e_idx.shape[0]
    counts = cnt[0, :n_exp].astype(jnp.int32)
    padded = (counts + tm - 1) // tm * tm
    ends = jnp.cumsum(padded)
    starts = ends - padded
    cstarts = jnp.cumsum(counts) - counts
    e_sel = e_idx[:, :MOE_TOP_K]
    r_sel = rank[:, :MOE_TOP_K]
    pos = (starts[e_sel] + r_sel).reshape(-1)
    token_of = jnp.arange(t * MOE_TOP_K, dtype=jnp.int32) // MOE_TOP_K
    _, sorted_tok = lax.sort_key_val((cstarts[e_sel] + r_sel).reshape(-1), token_of)
    n_tiles = t * MOE_TOP_K // tm + n_exp
    row0 = jnp.arange(n_tiles, dtype=jnp.int32) * tm
    tile_expert = jnp.minimum(jnp.sum(ends[None, :] <= row0[:, None], axis=1), n_exp - 1).astype(jnp.int32)
    n_used = (ends[-1:] // tm).astype(jnp.int32)
    local = row0 - starts[tile_expert]
    src0 = (cstarts[tile_expert] + local).astype(jnp.int32)
    valid = jnp.where(row0 < ends[-1], jnp.clip(counts[tile_expert] - local, 0, tm), 0).astype(jnp.int32)
    return pos, sorted_tok, tile_expert, n_used, src0, valid


GATHER_UNROLL = 8


def _experts_kernel(te_ref, nu_ref, src_ref, val_ref, tok_ref, h_ref, wg_ref, wu_ref, wd_ref, y_ref,
                    xbuf, wgu_sc, wd_sc, sems):
    i = pl.program_id(0)
    tm = xbuf.shape[1]
    f = wg_ref.shape[1]
    n_used = nu_ref[0]
    live = i < n_used
    slot = i % 2

    def row_copy(slot_, r, tok):
        return pltpu.make_async_copy(h_ref.at[pl.ds(tok, 1)], xbuf.at[slot_, pl.ds(r, 1)], sems.at[slot_])

    def gather(tile, slot_, wait):
        n = val_ref[tile]
        s0 = src_ref[tile]
        if not wait:
            @pl.when(n < tm)
            def _():
                xbuf[slot_] = jnp.zeros(xbuf.shape[1:], xbuf.dtype)

        def one(r):
            cp = row_copy(slot_, r, 0 if wait else tok_ref[s0 + r])
            if wait:
                cp.wait()
            else:
                cp.start()

        def chunk(c, carry):
            for u in range(GATHER_UNROLL):
                one(c * GATHER_UNROLL + u)
            return carry

        def single(r, carry):
            one(r)
            return carry

        whole = n // GATHER_UNROLL
        lax.fori_loop(0, whole, chunk, 0)
        lax.fori_loop(whole * GATHER_UNROLL, n, single, 0)

    @pl.when(i == 0)
    def _():
        gather(0, 0, False)

    @pl.when(i + 1 < n_used)
    def _():
        gather(i + 1, 1 - slot, False)

    fresh = jnp.logical_or(i == 0, te_ref[i] != te_ref[jnp.maximum(i - 1, 0)])

    @pl.when(jnp.logical_and(live, fresh))
    def _():
        wgu_sc[:, :f] = wg_ref[...].astype(wgu_sc.dtype)
        wgu_sc[:, f:] = wu_ref[...].astype(wgu_sc.dtype)
        wd_sc[...] = wd_ref[...].astype(wd_sc.dtype)

    @pl.when(live)
    def _():
        gather(i, slot, True)
        hgu = _dot(xbuf[slot].astype(wgu_sc.dtype), wgu_sc[...])
        hg = hgu[:, :f]
        hid = (hg * jax.nn.sigmoid(hg) * hgu[:, f:]).astype(wd_sc.dtype)
        y_ref[...] = _dot(hid, wd_sc[...])

    @pl.when(jnp.logical_not(live))
    def _():
        y_ref[...] = jnp.zeros_like(y_ref)


def _experts(h2d, sorted_tok, tile_expert, n_used, src0, valid, wg, wu, wd, layer, tm):
    t, d = h2d.shape
    f = wg.shape[3]
    n_tiles = tile_expert.shape[0]
    return pl.pallas_call(
        _experts_kernel,
        grid_spec=pltpu.PrefetchScalarGridSpec(
            num_scalar_prefetch=5,
            grid=(n_tiles,),
            in_specs=[pl.BlockSpec(memory_space=pl.ANY),
                      pl.BlockSpec((None, None, d, f), lambda i, te, *_: (layer, te[i], 0, 0)),
                      pl.BlockSpec((None, None, d, f), lambda i, te, *_: (layer, te[i], 0, 0)),
                      pl.BlockSpec((None, None, f, d), lambda i, te, *_: (layer, te[i], 0, 0))],
            out_specs=pl.BlockSpec((tm, d), lambda i, *_: (i, 0)),
            scratch_shapes=[pltpu.VMEM((2, tm, d), F32), pltpu.VMEM((d, 2 * f), BF16), pltpu.VMEM((f, d), BF16),
                            pltpu.SemaphoreType.DMA((2,))]),
        out_shape=jax.ShapeDtypeStruct((n_tiles * tm, d), F32),
        compiler_params=_cparams("arbitrary"),
        name="experts",
    )(tile_expert, n_used, src0, valid, sorted_tok, h2d, wg, wu, wd)


def _combine_kernel(pos_ref, g_ref, y_ref, o_ref, buf, sems, *, tt):
    i = pl.program_id(0)
    slot = i % 2

    def row_copy(slot_, t, k, p):
        return pltpu.make_async_copy(y_ref.at[pl.ds(p, 1)], buf.at[slot_, k, pl.ds(t, 1)], sems.at[slot_])

    def issue(tile, slot_):
        base = tile * (tt * MOE_TOP_K)

        def body(t, c):
            for k in range(MOE_TOP_K):
                row_copy(slot_, t, k, pos_ref[base + t * MOE_TOP_K + k]).start()
            return c

        lax.fori_loop(0, tt, body, 0)

    def drain(slot_):
        def body(t, c):
            for k in range(MOE_TOP_K):
                row_copy(slot_, 0, 0, 0).wait()
            return c

        lax.fori_loop(0, tt, body, 0)

    @pl.when(i == 0)
    def _():
        issue(0, 0)

    @pl.when(i + 1 < pl.num_programs(0))
    def _():
        issue(i + 1, 1 - slot)

    drain(slot)
    g = g_ref[...]
    acc = g[:, 0:1] * buf[slot, 0]
    for k in range(1, MOE_TOP_K):
        acc = acc + g[:, k:k + 1] * buf[slot, k]
    o_ref[...] = acc


def _combine(y, pos, gates):
    t = gates.shape[0]
    d = y.shape[1]
    tt = _tile(t, 64)
    kern = functools.partial(_combine_kernel, tt=tt)
    return pl.pallas_call(
        kern,
        grid_spec=pltpu.PrefetchScalarGridSpec(
            num_scalar_prefetch=1,
            grid=(t // tt,),
            in_specs=[pl.BlockSpec((tt, LANES), lambda i, *_: (i, 0)),
                      pl.BlockSpec(memory_space=pl.ANY)],
            out_specs=pl.BlockSpec((tt, d), lambda i, *_: (i, 0)),
            scratch_shapes=[pltpu.VMEM((2, MOE_TOP_K, tt, d), F32), pltpu.SemaphoreType.DMA((2,))]),
        out_shape=jax.ShapeDtypeStruct((t, d), F32),
        compiler_params=_cparams("arbitrary"),
        name="moe_combine",
    )(pos, gates, y)


def _shared_kernel(x_ref, wg_ref, wu_ref, wd_ref, o_ref, wgu_sc, wd_sc):
    f = wg_ref.shape[1]

    @pl.when(pl.program_id(0) == 0)
    def _():
        wgu_sc[:, :f] = wg_ref[...].astype(wgu_sc.dtype)
        wgu_sc[:, f:] = wu_ref[...].astype(wgu_sc.dtype)
        wd_sc[...] = wd_ref[...].astype(wd_sc.dtype)

    hgu = _dot(x_ref[...], wgu_sc[...])
    hg = hgu[:, :f]
    hid = (hg * jax.nn.sigmoid(hg) * hgu[:, f:]).astype(wd_sc.dtype)
    o_ref[...] = _dot(hid, wd_sc[...])


def _shared_expert(h2d, wg, wu, wd, layer):
    t, d = h2d.shape
    f = wg.shape[2]
    tm = _tile(t, 512)
    return pl.pallas_call(
        _shared_kernel,
        grid=(t // tm,),
        in_specs=[pl.BlockSpec((tm, d), lambda i: (i, 0)),
                  pl.BlockSpec((None, d, f), lambda i: (layer, 0, 0)),
                  pl.BlockSpec((None, d, f), lambda i: (layer, 0, 0)),
                  pl.BlockSpec((None, f, d), lambda i: (layer, 0, 0))],
        out_specs=pl.BlockSpec((tm, d), lambda i: (i, 0)),
        out_shape=jax.ShapeDtypeStruct((t, d), F32),
        scratch_shapes=[pltpu.VMEM((d, 2 * f), BF16), pltpu.VMEM((f, d), BF16)],
        compiler_params=_cparams("arbitrary"),
        name="shared_expert",
    )(h2d, wg, wu, wd)


EXPERT_ROW_TILE = 256


def _moe(h, h32, layer, router_w, router_b, wg, wu, wd, sg, su, sd):
    b, s, d = h.shape
    t = b * s
    n_exp = wg.shape[1]
    tm = EXPERT_ROW_TILE
    h2d = h.reshape(t, d)
    e_idx, gates, rank, cnt = _router(h2d, router_w[layer], router_b[layer])
    pos, sorted_tok, tile_expert, n_used, src0, valid = _route_plan(e_idx, rank, cnt, n_exp, tm)
    y = _experts(h32.reshape(t, d), sorted_tok, tile_expert, n_used, src0, valid, wg, wu, wd, layer, tm)
    routed = _combine(y, pos, gates)
    shared = _shared_expert(h2d, sg, su, sd, layer)
    return routed.reshape(b, s, d), shared.reshape(b, s, d)


def kernel(x, c, rel_bias, ada_w, ada_b, a_w_qkv, a_lambda, a_subln_g, a_w_o, kv_ada_w, kv_ada_b, kv_w,
           b_w_q, b_w_o, router_w, router_b, e_w_gate, e_w_up, e_w_down, s_w_gate, s_w_up, s_w_down, final_g):
    b, s, d = x.shape
    depth = ada_w.shape[0]
    n_a = a_w_qkv.shape[0]
    a_heads = a_w_qkv.shape[2] // (3 * 2 * A_HEAD_DIM)
    b_heads = rel_bias.shape[1]
    b_head_dim = b_w_q.shape[2] // b_heads
    assert 2 * a_heads == b_heads and 1 <= n_a < depth and b_head_dim == A_HEAD_DIM

    c_pad = jnp.pad(c, ((0, 8 - b), (0, 0)))

    def ada(w, layer, bias, parts):
        return _ada_proj(c_pad, w, layer, bias)[:b].reshape(b, parts, 1, d)

    bias_tiles = _bias_tiles(rel_bias, MOBA_BLOCK, s, A_HEAD_DIM ** -0.5)

    def vec(m, k):
        return m[:, k]

    mod = ada(ada_w, 0, ada_b[0], 6)
    (h,) = _resmod(x, [], None, [(vec(mod, 0), vec(mod, 1), True)], False, [BF16])
    kv = None
    for layer in range(depth):
        if layer < n_a:
            lambda_init = 0.8 - 0.6 * math.exp(-0.3 * layer)
            qkv = _matmul(h.reshape(b * s, d), a_w_qkv, layer, BF16)
            o = _diff_attention(qkv.reshape(b, s, -1), a_lambda[layer], a_subln_g[layer], bias_tiles,
                                a_heads, lambda_init)
            w_o, w_o_layer = a_w_o, layer
        else:
            j = layer - n_a
            hq = h
            if kv is None:
                hq, hkv = h
                kv = _matmul(hkv.reshape(b * s, d), kv_w[None], 0, BF16).reshape(b, s, -1)
            q = _matmul(hq.reshape(b * s, d), b_w_q, j, BF16).reshape(b, s, -1)
            o = _moba_attention(q, kv, bias_tiles, b_heads)
            w_o, w_o_layer = b_w_o, j
        mix = _matmul(o.reshape(b * s, -1), w_o, w_o_layer, F32).reshape(b, s, d)
        ffn_mod = (vec(mod, 3), vec(mod, 4), True)
        x, h, h32 = _resmod(x, [mix], vec(mod, 2), [ffn_mod, ffn_mod], True, [BF16, F32])
        routed, shared = _moe(h, h32, layer, router_w, router_b, e_w_gate, e_w_up, e_w_down,
                              s_w_gate, s_w_up, s_w_down)
        g_ffn = vec(mod, 5)
        if layer + 1 == depth:
            zero = jnp.zeros((1, 1, d), F32)
            (out,) = _resmod(x, [routed, shared], g_ffn, [(zero, final_g.reshape(1, 1, d), False)], False, [F32])
            return out
        mod = ada(ada_w, layer + 1, ada_b[layer + 1], 6)
        mods = [(vec(mod, 0), vec(mod, 1), True)]
        if layer + 1 == n_a:
            kvmod = ada(kv_ada_w[None], 0, kv_ada_b, 2)
            mods.append((vec(kvmod, 0), vec(kvmod, 1), True))
            x, hq, hkv = _resmod(x, [routed, shared], g_ffn, mods, True, [BF16, BF16])
            h = (hq, hkv)
        else:
            x, h = _resmod(x, [routed, shared], g_ffn, mods, True, [BF16])
```

```python
import functools
import math

import numpy as np
import jax
import jax.numpy as jnp
from jax import lax
from jax.experimental import pallas as pl
from jax.experimental.pallas import tpu as pltpu

F32 = jnp.float32
BF16 = jnp.bfloat16

A_HEAD_DIM = 128
MOBA_BLOCK = 256
MOBA_TOPK = 3
REL_BUCKETS = 32
REL_MAX_DIST = 128
MOE_TOP_K = 8
ROUTED_SCALE = 2.5
NORM_EPS = 1e-6
SUBLN_EPS = 1e-5

LANES = 128
SUBLANES = 8
VMEM_LIMIT = 56 * 1024 * 1024
LOG2E = math.log2(math.e)


def _cparams(*sem):
    return pltpu.CompilerParams(dimension_semantics=sem, vmem_limit_bytes=VMEM_LIMIT)


def _tile(n, pref):
    t = min(pref, n)
    while n % t:
        t //= 2
    return t


def _dot(a, b):
    return jnp.dot(a, b, preferred_element_type=F32)


def _dot_t(a, b):
    return lax.dot_general(a, b, (((1,), (1,)), ((), ())), preferred_element_type=F32)


def _ada_kernel(c_ref, w_ref, b_ref, o_ref):
    c = c_ref[...]
    s = (c * jax.nn.sigmoid(c)).astype(BF16)
    o_ref[...] = _dot(s, w_ref[...].astype(BF16)) + b_ref[...]


def _ada_proj(c_pad, w, layer, b):
    _, d, n = w.shape
    tn = _tile(n, 512)
    return pl.pallas_call(
        _ada_kernel,
        grid=(n // tn,),
        in_specs=[pl.BlockSpec((8, d), lambda j: (0, 0)),
                  pl.BlockSpec((None, d, tn), lambda j: (layer, 0, j)),
                  pl.BlockSpec((1, tn), lambda j: (0, j))],
        out_specs=pl.BlockSpec((8, tn), lambda j: (0, j)),
        out_shape=jax.ShapeDtypeStruct((8, n), F32),
        compiler_params=_cparams("parallel"),
        name="ada_proj",
    )(c_pad, w, b.reshape(1, n))


def _resmod_kernel(*refs, n_add, n_mod, emit_x, add_one):
    x_ref = refs[0]
    add_refs = refs[1:1 + n_add]
    pos = 1 + n_add
    x = x_ref[0]
    if n_add:
        g = refs[pos][0]
        pos += 1
        tot = add_refs[0][0].astype(F32)
        for r in add_refs[1:]:
            tot = tot + r[0].astype(F32)
        x = x + g * tot
    mod_refs = refs[pos:pos + 2 * n_mod]
    out_refs = refs[pos + 2 * n_mod:]
    oi = 0
    if emit_x:
        out_refs[0][0] = x
        oi = 1
    y = x * lax.rsqrt(jnp.mean(x * x, axis=-1, keepdims=True) + NORM_EPS)
    for k in range(n_mod):
        sh = mod_refs[2 * k][0]
        sc = mod_refs[2 * k + 1][0]
        if add_one[k]:
            sc = 1.0 + sc
        out_refs[oi + k][0] = (y * sc + sh).astype(out_refs[oi + k].dtype)


def _resmod(x, addends, gate, mods, emit_x, out_dtypes):
    b, s, d = x.shape
    ts = _tile(s, 128)
    row = pl.BlockSpec((1, ts, d), lambda i, j: (i, j, 0))

    def vec_spec(v):
        if v.shape[0] == 1:
            return pl.BlockSpec((1, 1, d), lambda i, j: (0, 0, 0))
        return pl.BlockSpec((1, 1, d), lambda i, j: (i, 0, 0))

    args = [x] + list(addends)
    specs = [row] * (1 + len(addends))
    if addends:
        args.append(gate)
        specs.append(vec_spec(gate))
    for sh, sc, _ in mods:
        args += [sh, sc]
        specs += [vec_spec(sh), vec_spec(sc)]
    out_shape, out_specs = [], []
    if emit_x:
        out_shape.append(jax.ShapeDtypeStruct((b, s, d), F32))
        out_specs.append(row)
    for dt in out_dtypes:
        out_shape.append(jax.ShapeDtypeStruct((b, s, d), dt))
        out_specs.append(row)
    kern = functools.partial(_resmod_kernel, n_add=len(addends), n_mod=len(mods), emit_x=emit_x,
                             add_one=tuple(m[2] for m in mods))
    return pl.pallas_call(
        kern, grid=(b, s // ts), in_specs=specs, out_specs=out_specs, out_shape=out_shape,
        compiler_params=_cparams("parallel", "parallel"), name="resmod",
    )(*args)


def _mm_kernel(a_ref, w_ref, o_ref, wb_sc):
    @pl.when(pl.program_id(1) == 0)
    def _():
        wb_sc[...] = w_ref[...].astype(wb_sc.dtype)

    o_ref[...] = _dot(a_ref[...], wb_sc[...]).astype(o_ref.dtype)


def _matmul(a, w, layer, out_dtype):
    m, k = a.shape
    n = w.shape[2]
    tm, tn = _tile(m, 1024), _tile(n, 512)
    return pl.pallas_call(
        _mm_kernel,
        grid=(n // tn, m // tm),
        in_specs=[pl.BlockSpec((tm, k), lambda j, i: (i, 0)),
                  pl.BlockSpec((None, k, tn), lambda j, i: (layer, 0, j))],
        out_specs=pl.BlockSpec((tm, tn), lambda j, i: (i, j)),
        out_shape=jax.ShapeDtypeStruct((m, n), out_dtype),
        scratch_shapes=[pltpu.VMEM((k, tn), BF16)],
        compiler_params=_cparams("parallel", "arbitrary"),
        name="proj",
    )(a, w)


def _t5_bucket_np(dist):
    n = np.maximum(dist, 0)
    max_exact = REL_BUCKETS // 2
    nf = np.maximum(n, 1).astype(np.float32)
    large = max_exact + (np.log(nf / np.float32(max_exact)) / np.float32(math.log(REL_MAX_DIST / max_exact))
                         * np.float32(REL_BUCKETS - max_exact)).astype(np.int32)
    large = np.minimum(large, REL_BUCKETS - 1)
    return np.where(n < max_exact, n, large).astype(np.int32)


def _bias_tile_kernel(rel_ref, bk_ref, o_ref, *, far_bucket, inv_scale):
    c = pl.program_id(0)
    bk = bk_ref[...]
    acc = jnp.zeros(bk.shape, F32)
    for bidx in range(REL_BUCKETS):
        acc = jnp.where(bk == bidx, rel_ref[bidx, c], acc)
    o_ref[0] = (acc - rel_ref[far_bucket, c]) * inv_scale


def _bias_tiles(rel_bias, t, seq, scale):
    i = np.arange(t)[:, None]
    j = np.arange(t)[None, :]
    bk = np.stack([_t5_bucket_np(i - j), _t5_bucket_np(i - j + t)])
    far = _t5_bucket_np(np.arange(t + 1, max(seq, t + 2)))
    assert (far == far[0]).all(), "distances beyond two tiles must share one bucket"
    ncol = rel_bias.shape[1]
    kern = functools.partial(_bias_tile_kernel, far_bucket=int(far[0]), inv_scale=1.0 / scale)
    return pl.pallas_call(
        kern,
        grid=(ncol,),
        in_specs=[pl.BlockSpec(memory_space=pltpu.SMEM),
                  pl.BlockSpec((2, t, t), lambda c: (0, 0, 0))],
        out_specs=pl.BlockSpec((1, 2, t, t), lambda c: (c, 0, 0, 0)),
        out_shape=jax.ShapeDtypeStruct((ncol, 2, t, t), F32),
        compiler_params=_cparams("parallel"),
        name="bias_tiles",
    )(rel_bias, jnp.asarray(bk))


def _lane_fold(x, op, part):
    for c in range(x.shape[1] // LANES):
        piece = x[:, c * LANES:(c + 1) * LANES]
        part = piece if part is None else op(part, piece)
    return part


def _softmax_rows(s_sc, p_sc, n_cols, c2):
    tile = s_sc.shape[0]
    mpart = None
    for c0 in range(0, n_cols, tile):
        mpart = _lane_fold(s_sc[:, c0:c0 + tile], jnp.maximum, mpart)
    m = jnp.max(mpart, axis=1, keepdims=True)
    lpart = None
    for c0 in range(0, n_cols, tile):
        p = jnp.exp2((s_sc[:, c0:c0 + tile] - m) * c2)
        p_sc[:, c0:c0 + tile] = p.astype(p_sc.dtype)
        lpart = _lane_fold(p, jnp.add, lpart)
    return jnp.sum(lpart, axis=1, keepdims=True)


def _diff_attn_kernel(lam_ref, g_ref, q_ref, k_ref, v_ref, bias_ref, o_ref, s0, s1, p0, p1, *, t, lambda_init):
    seq = q_ref.shape[1]
    dh = A_HEAD_DIM
    c2 = dh ** -0.5 * LOG2E
    lv = lam_ref[...]
    lam = (jnp.exp(jnp.sum(lv[0:1] * lv[1:2], axis=1, keepdims=True))
           - jnp.exp(jnp.sum(lv[2:3] * lv[3:4], axis=1, keepdims=True)) + lambda_init)
    causal = (lax.broadcasted_iota(jnp.int32, (t, t), 0) >= lax.broadcasted_iota(jnp.int32, (t, t), 1))
    for qi in range(seq // t):
        r0 = qi * t
        n_cols = r0 + t
        heads = []
        for mp, (s_sc, p_sc) in enumerate(((s0, p0), (s1, p1))):
            q = q_ref[0, r0:r0 + t, mp * dh:(mp + 1) * dh]
            for j in range(qi + 1):
                s = _dot_t(q, k_ref[0, j * t:(j + 1) * t, mp * dh:(mp + 1) * dh])
                if j == qi:
                    s = jnp.where(causal, s + bias_ref[0, mp, 0], -jnp.inf)
                elif j == qi - 1:
                    s = s + bias_ref[0, mp, 1]
                s_sc[:, j * t:(j + 1) * t] = s
            l = _softmax_rows(s_sc, p_sc, n_cols, c2)
            heads.append(_dot(p_sc[:, :n_cols], v_ref[0, :n_cols, :]) / l)
        o = heads[0] - lam * heads[1]
        o = o * lax.rsqrt(jnp.mean(o * o, axis=-1, keepdims=True) + SUBLN_EPS)
        o_ref[0, r0:r0 + t, :] = (o * g_ref[...] * (1.0 - lambda_init)).astype(o_ref.dtype)


def _diff_attention(qkv, lam_vecs, subln_g, bias_tiles, n_heads, lambda_init):
    b, s, w3 = qkv.shape
    w = w3 // 3
    hw = 2 * A_HEAD_DIM
    t = bias_tiles.shape[-1]
    bt = bias_tiles.reshape(n_heads, 2, 2, t, t)
    kern = functools.partial(_diff_attn_kernel, t=t, lambda_init=lambda_init)
    return pl.pallas_call(
        kern,
        grid=(b, n_heads),
        in_specs=[pl.BlockSpec((4, A_HEAD_DIM), lambda bi, h: (0, 0)),
                  pl.BlockSpec((1, hw), lambda bi, h: (0, 0)),
                  pl.BlockSpec((1, s, hw), lambda bi, h: (bi, 0, h)),
                  pl.BlockSpec((1, s, hw), lambda bi, h: (bi, 0, n_heads + h)),
                  pl.BlockSpec((1, s, hw), lambda bi, h: (bi, 0, 2 * n_heads + h)),
                  pl.BlockSpec((1, 2, 2, t, t), lambda bi, h: (h, 0, 0, 0, 0))],
        out_specs=pl.BlockSpec((1, s, hw), lambda bi, h: (bi, 0, h)),
        out_shape=jax.ShapeDtypeStruct((b, s, w), BF16),
        scratch_shapes=[pltpu.VMEM((t, s), F32), pltpu.VMEM((t, s), F32),
                        pltpu.VMEM((t, s), BF16), pltpu.VMEM((t, s), BF16)],
        compiler_params=_cparams("parallel", "parallel"),
        name="diff_attn",
    )(lam_vecs, subln_g.reshape(1, hw), qkv, qkv, qkv, bt)


def _moba_kernel(q_ref, k_ref, v_ref, bias_ref, o_ref, s_sc, p_sc, mask_sc, *, t):
    seq = q_ref.shape[1]
    dh = q_ref.shape[2]
    nb = seq // t
    c2 = dh ** -0.5 * LOG2E
    causal = (lax.broadcasted_iota(jnp.int32, (t, t), 0) >= lax.broadcasted_iota(jnp.int32, (t, t), 1))
    r = lax.broadcasted_iota(jnp.int32, (LANES, seq), 0)
    c = lax.broadcasted_iota(jnp.int32, (LANES, seq), 1)
    ind = jnp.where(c >= r * t, jnp.where(c < (r + 1) * t, 1.0, 0.0), 0.0).astype(k_ref.dtype)
    kmean = (_dot(ind, k_ref[0]) * (1.0 / t)).astype(k_ref.dtype)
    lane = lax.broadcasted_iota(jnp.int32, (t, LANES), 1)
    for qi in range(nb):
        r0 = qi * t
        n_cols = r0 + t
        q = q_ref[0, r0:r0 + t, :]
        masked = qi > MOBA_TOPK
        if masked:
            gate = jnp.where(lane < qi, _dot_t(q, kmean), -jnp.inf)
            for n in range(qi):
                gn = gate[:, n:n + 1]
                ahead = jnp.where(gate > gn, 1.0, jnp.where(lane < n, jnp.where(gate == gn, 1.0, 0.0), 0.0))
                cnt = jnp.sum(ahead, axis=1, keepdims=True)
                mask_sc[n] = jnp.broadcast_to(jnp.where(cnt < MOBA_TOPK, 0.0, -jnp.inf), (t, LANES))
        for j in range(qi + 1):
            s = _dot_t(q, k_ref[0, j * t:(j + 1) * t, :])
            if j == qi:
                s = jnp.where(causal, s + bias_ref[0, 0], -jnp.inf)
            else:
                if j == qi - 1:
                    s = s + bias_ref[0, 1]
                if masked:
                    mk = mask_sc[j]
                    s = s + jnp.concatenate([mk] * (t // LANES), axis=1)
            s_sc[:, j * t:(j + 1) * t] = s
        l = _softmax_rows(s_sc, p_sc, n_cols, c2)
        o_ref[0, r0:r0 + t, :] = (_dot(p_sc[:, :n_cols], v_ref[0, :n_cols, :]) / l).astype(o_ref.dtype)


def _moba_attention(q, kv, bias_tiles, n_heads):
    b, s, w = q.shape
    dh = w // n_heads
    t = MOBA_BLOCK
    nb = s // t
    assert s % t == 0 and bias_tiles.shape[-1] == t and nb <= LANES
    kern = functools.partial(_moba_kernel, t=t)
    return pl.pallas_call(
        kern,
        grid=(b, n_heads),
        in_specs=[pl.BlockSpec((1, s, dh), lambda bi, h: (bi, 0, h)),
                  pl.BlockSpec((1, s, dh), lambda bi, h: (bi, 0, h)),
                  pl.BlockSpec((1, s, dh), lambda bi, h: (bi, 0, n_heads + h)),
                  pl.BlockSpec((1, 2, t, t), lambda bi, h: (h, 0, 0, 0))],
        out_specs=pl.BlockSpec((1, s, dh), lambda bi, h: (bi, 0, h)),
        out_shape=jax.ShapeDtypeStruct((b, s, w), BF16),
        scratch_shapes=[pltpu.VMEM((t, s), F32), pltpu.VMEM((t, s), BF16), pltpu.VMEM((nb, t, LANES), F32)],
        compiler_params=_cparams("parallel", "parallel"),
        name="moba_attn",
    )(q, kv, kv, bias_tiles)


def _top_experts(h_ref, w_ref, b_ref):
    aff = jax.nn.sigmoid(_dot(h_ref[...], w_ref[...]))
    work = aff + b_ref[...]
    lane = lax.broadcasted_iota(jnp.int32, aff.shape, 1).astype(F32)
    mask = jnp.zeros(aff.shape, F32)
    firsts, affs = [], []
    for _ in range(MOE_TOP_K):
        mx = jnp.max(work, axis=1, keepdims=True)
        first = jnp.min(jnp.where(work == mx, lane, float(LANES)), axis=1, keepdims=True)
        pick = lane == first
        affs.append(jnp.sum(jnp.where(pick, aff, 0.0), axis=1, keepdims=True))
        firsts.append(first)
        mask = jnp.where(pick, 1.0, mask)
        work = jnp.where(pick, -jnp.inf, work)
    return lane, mask, firsts, affs


def _router_count_kernel(h_ref, w_ref, b_ref, cnt_ref):
    @pl.when(pl.program_id(0) == 0)
    def _():
        cnt_ref[...] = jnp.zeros_like(cnt_ref)

    _, mask, _, _ = _top_experts(h_ref, w_ref, b_ref)
    cnt_ref[...] += jnp.broadcast_to(jnp.sum(mask, axis=0, keepdims=True), cnt_ref.shape)


def _router_kernel(h_ref, w_ref, b_ref, st_ref, cst_ref, g_ref, pos_ref, cpos_ref, carry_sc):
    @pl.when(pl.program_id(0) == 0)
    def _():
        carry_sc[...] = jnp.zeros_like(carry_sc)

    lane, mask, firsts, affs = _top_experts(h_ref, w_ref, b_ref)
    tm = mask.shape[0]
    denom = affs[0]
    for a in affs[1:]:
        denom = denom + a
    tri = jnp.where(lax.broadcasted_iota(jnp.int32, (tm, tm), 0) > lax.broadcasted_iota(jnp.int32, (tm, tm), 1),
                    1.0, 0.0).astype(BF16)
    carry = carry_sc[...]
    rank = _dot(tri, mask.astype(BF16)) + carry
    carry_sc[...] = carry + jnp.sum(mask, axis=0, keepdims=True)
    row = rank + st_ref[...]
    crow = rank + cst_ref[...]
    g_out = jnp.zeros(mask.shape, F32)
    p_out = jnp.zeros(mask.shape, F32)
    c_out = jnp.zeros(mask.shape, F32)
    for k in range(MOE_TOP_K):
        slot = lane == float(k)
        mine = lane == firsts[k]
        g_out = jnp.where(slot, affs[k] / denom * ROUTED_SCALE, g_out)
        p_out = jnp.where(slot, jnp.sum(jnp.where(mine, row, 0.0), axis=1, keepdims=True), p_out)
        c_out = jnp.where(slot, jnp.sum(jnp.where(mine, crow, 0.0), axis=1, keepdims=True), c_out)
    g_ref[...] = g_out
    pos_ref[...] = p_out.astype(jnp.int32)
    cpos_ref[...] = c_out.astype(jnp.int32)


def _route(h2d, router_w, router_b, tm_rows):
    t, d = h2d.shape
    n_exp = router_w.shape[1]
    assert n_exp <= LANES and MOE_TOP_K <= LANES and t * MOE_TOP_K + n_exp * tm_rows < 2 ** 24
    w = jnp.pad(router_w, ((0, 0), (0, LANES - n_exp))).astype(BF16)
    bias = jnp.pad(router_b, (0, LANES - n_exp), constant_values=-jnp.inf).reshape(1, LANES)
    tm = _tile(t, 256)
    tok = pl.BlockSpec((tm, LANES), lambda i: (i, 0))
    vec = pl.BlockSpec((1, LANES), lambda i: (0, 0))
    in_specs = [pl.BlockSpec((tm, d), lambda i: (i, 0)), pl.BlockSpec((d, LANES), lambda i: (0, 0)), vec]
    cnt = pl.pallas_call(
        _router_count_kernel,
        grid=(t // tm,),
        in_specs=in_specs,
        out_specs=pl.BlockSpec((SUBLANES, LANES), lambda i: (0, 0)),
        out_shape=jax.ShapeDtypeStruct((SUBLANES, LANES), F32),
        compiler_params=_cparams("arbitrary"),
        name="router_count",
    )(h2d, w, bias)
    counts = cnt[0, :n_exp].astype(jnp.int32)
    padded = (counts + tm_rows - 1) // tm_rows * tm_rows
    ends = jnp.cumsum(padded)
    starts = ends - padded
    cstarts = jnp.cumsum(counts) - counts

    def lanes(v):
        return jnp.pad(v.astype(F32), (0, LANES - n_exp)).reshape(1, LANES)

    gates, pos, cpos = pl.pallas_call(
        _router_kernel,
        grid=(t // tm,),
        in_specs=in_specs + [vec, vec],
        out_specs=[tok, tok, tok],
        out_shape=[jax.ShapeDtypeStruct((t, LANES), F32), jax.ShapeDtypeStruct((t, LANES), jnp.int32),
                   jax.ShapeDtypeStruct((t, LANES), jnp.int32)],
        scratch_shapes=[pltpu.VMEM((1, LANES), F32)],
        compiler_params=_cparams("arbitrary"),
        name="router",
    )(h2d, w, bias, lanes(starts), lanes(cstarts))
    pos = pos[:, :MOE_TOP_K].reshape(-1)
    token_of = jnp.arange(t * MOE_TOP_K, dtype=jnp.int32) // MOE_TOP_K
    _, sorted_tok = lax.sort_key_val(cpos[:, :MOE_TOP_K].reshape(-1), token_of)
    sorted_tok = jnp.concatenate([sorted_tok, jnp.zeros((tm_rows,), jnp.int32)])
    n_tiles = t * MOE_TOP_K // tm_rows + n_exp
    row0 = jnp.arange(n_tiles, dtype=jnp.int32) * tm_rows
    tile_expert = jnp.minimum(jnp.sum(ends[None, :] <= row0[:, None], axis=1), n_exp - 1).astype(jnp.int32)
    n_used = (ends[-1:] // tm_rows).astype(jnp.int32)
    src0 = cstarts[tile_expert] + row0 - starts[tile_expert]
    src0 = jnp.where(row0 < ends[-1], src0, 0).astype(jnp.int32)
    return gates, pos, sorted_tok, tile_expert, n_used, src0


def _experts_kernel(te_ref, nu_ref, src_ref, tok_ref, h_ref, wg_ref, wu_ref, wd_ref, y_ref,
                    xbuf, xb_sc, wgu_sc, wd_sc, sems):
    i = pl.program_id(0)
    tm = xbuf.shape[1]
    f = wg_ref.shape[1]
    n_used = nu_ref[0]
    live = i < n_used
    has_next = i + 1 < n_used
    slot = i % 2

    def row_copy(slot_, r, tok):
        return pltpu.make_async_copy(h_ref.at[pl.ds(tok, 1)], xbuf.at[slot_, pl.ds(r, 1)], sems.at[slot_])

    def issue(tile, slot_):
        s0 = src_ref[tile]
        for r in range(tm):
            row_copy(slot_, r, tok_ref[s0 + r]).start()

    def drain(slot_):
        for r in range(tm):
            row_copy(slot_, r, 0).wait()

    def stage():
        xb_sc[...] = xbuf[slot].astype(xb_sc.dtype)

    def compute():
        hgu = _dot(xb_sc[...], wgu_sc[...])
        hg = hgu[:, :f]
        hid = (hg * jax.nn.sigmoid(hg) * hgu[:, f:]).astype(wd_sc.dtype)
        y_ref[...] = _dot(hid, wd_sc[...])

    @pl.when(i == 0)
    def _():
        issue(0, 0)

    fresh = jnp.logical_or(i == 0, te_ref[i] != te_ref[jnp.maximum(i - 1, 0)])

    @pl.when(jnp.logical_and(live, fresh))
    def _():
        wgu_sc[:, :f] = wg_ref[...].astype(wgu_sc.dtype)
        wgu_sc[:, f:] = wu_ref[...].astype(wgu_sc.dtype)
        wd_sc[...] = wd_ref[...].astype(wd_sc.dtype)

    @pl.when(jnp.logical_and(live, has_next))
    def _():
        drain(slot)
        stage()
        issue(i + 1, 1 - slot)
        compute()

    @pl.when(jnp.logical_and(live, jnp.logical_not(has_next)))
    def _():
        drain(slot)
        stage()
        compute()

    @pl.when(jnp.logical_not(live))
    def _():
        y_ref[...] = jnp.zeros_like(y_ref)


def _experts(h2d, sorted_tok, tile_expert, n_used, src0, wg, wu, wd, layer, tm):
    t, d = h2d.shape
    f = wg.shape[3]
    n_tiles = tile_expert.shape[0]
    return pl.pallas_call(
        _experts_kernel,
        grid_spec=pltpu.PrefetchScalarGridSpec(
            num_scalar_prefetch=4,
            grid=(n_tiles,),
            in_specs=[pl.BlockSpec(memory_space=pl.ANY),
                      pl.BlockSpec((None, None, d, f), lambda i, te, *_: (layer, te[i], 0, 0)),
                      pl.BlockSpec((None, None, d, f), lambda i, te, *_: (layer, te[i], 0, 0)),
                      pl.BlockSpec((None, None, f, d), lambda i, te, *_: (layer, te[i], 0, 0))],
            out_specs=pl.BlockSpec((tm, d), lambda i, *_: (i, 0)),
            scratch_shapes=[pltpu.VMEM((2, tm, d), F32), pltpu.VMEM((tm, d), BF16), pltpu.VMEM((d, 2 * f), BF16),
                            pltpu.VMEM((f, d), BF16), pltpu.SemaphoreType.DMA((2,))]),
        out_shape=jax.ShapeDtypeStruct((n_tiles * tm, d), F32),
        compiler_params=_cparams("arbitrary"),
        name="experts",
    )(tile_expert, n_used, src0, sorted_tok, h2d, wg, wu, wd)


def _combine_kernel(pos_ref, g_ref, y_ref, o_ref, buf, sems, *, tt):
    i = pl.program_id(0)
    slot = i % 2

    def row_copy(slot_, t, k, p):
        return pltpu.make_async_copy(y_ref.at[pl.ds(p, 1)], buf.at[slot_, k, pl.ds(t, 1)], sems.at[slot_])

    def issue(tile, slot_):
        base = tile * (tt * MOE_TOP_K)

        def body(t, c):
            for k in range(MOE_TOP_K):
                row_copy(slot_, t, k, pos_ref[base + t * MOE_TOP_K + k]).start()
            return c

        lax.fori_loop(0, tt, body, 0)

    def drain(slot_):
        def body(t, c):
            for k in range(MOE_TOP_K):
                row_copy(slot_, 0, 0, 0).wait()
            return c

        lax.fori_loop(0, tt, body, 0)

    @pl.when(i == 0)
    def _():
        issue(0, 0)

    @pl.when(i + 1 < pl.num_programs(0))
    def _():
        issue(i + 1, 1 - slot)

    drain(slot)
    g = g_ref[...]
    acc = g[:, 0:1] * buf[slot, 0]
    for k in range(1, MOE_TOP_K):
        acc = acc + g[:, k:k + 1] * buf[slot, k]
    o_ref[...] = acc


def _combine(y, pos, gates):
    t = gates.shape[0]
    d = y.shape[1]
    tt = _tile(t, 64)
    kern = functools.partial(_combine_kernel, tt=tt)
    return pl.pallas_call(
        kern,
        grid_spec=pltpu.PrefetchScalarGridSpec(
            num_scalar_prefetch=1,
            grid=(t // tt,),
            in_specs=[pl.BlockSpec((tt, LANES), lambda i, *_: (i, 0)),
                      pl.BlockSpec(memory_space=pl.ANY)],
            out_specs=pl.BlockSpec((tt, d), lambda i, *_: (i, 0)),
            scratch_shapes=[pltpu.VMEM((2, MOE_TOP_K, tt, d), F32), pltpu.SemaphoreType.DMA((2,))]),
        out_shape=jax.ShapeDtypeStruct((t, d), F32),
        compiler_params=_cparams("arbitrary"),
        name="moe_combine",
    )(pos, gates, y)


def _shared_kernel(x_ref, wg_ref, wu_ref, wd_ref, o_ref, wgu_sc, wd_sc):
    f = wg_ref.shape[1]

    @pl.when(pl.program_id(0) == 0)
    def _():
        wgu_sc[:, :f] = wg_ref[...].astype(wgu_sc.dtype)
        wgu_sc[:, f:] = wu_ref[...].astype(wgu_sc.dtype)
        wd_sc[...] = wd_ref[...].astype(wd_sc.dtype)

    hgu = _dot(x_ref[...], wgu_sc[...])
    hg = hgu[:, :f]
    hid = (hg * jax.nn.sigmoid(hg) * hgu[:, f:]).astype(wd_sc.dtype)
    o_ref[...] = _dot(hid, wd_sc[...])


def _shared_expert(h2d, wg, wu, wd, layer):
    t, d = h2d.shape
    f = wg.shape[2]
    tm = _tile(t, 512)
    return pl.pallas_call(
        _shared_kernel,
        grid=(t // tm,),
        in_specs=[pl.BlockSpec((tm, d), lambda i: (i, 0)),
                  pl.BlockSpec((None, d, f), lambda i: (layer, 0, 0)),
                  pl.BlockSpec((None, d, f), lambda i: (layer, 0, 0)),
                  pl.BlockSpec((None, f, d), lambda i: (layer, 0, 0))],
        out_specs=pl.BlockSpec((tm, d), lambda i: (i, 0)),
        out_shape=jax.ShapeDtypeStruct((t, d), F32),
        scratch_shapes=[pltpu.VMEM((d, 2 * f), BF16), pltpu.VMEM((f, d), BF16)],
        compiler_params=_cparams("arbitrary"),
        name="shared_expert",
    )(h2d, wg, wu, wd)


EXPERT_ROW_TILE = 256


def _moe(h, h32, layer, router_w, router_b, wg, wu, wd, sg, su, sd):
    b, s, d = h.shape
    t = b * s
    tm = EXPERT_ROW_TILE
    h2d = h.reshape(t, d)
    gates, pos, sorted_tok, tile_expert, n_used, src0 = _route(h2d, router_w[layer], router_b[layer], tm)
    y = _experts(h32.reshape(t, d), sorted_tok, tile_expert, n_used, src0, wg, wu, wd, layer, tm)
    routed = _combine(y, pos, gates)
    shared = _shared_expert(h2d, sg, su, sd, layer)
    return routed.reshape(b, s, d), shared.reshape(b, s, d)


def kernel(x, c, rel_bias, ada_w, ada_b, a_w_qkv, a_lambda, a_subln_g, a_w_o, kv_ada_w, kv_ada_b, kv_w,
           b_w_q, b_w_o, router_w, router_b, e_w_gate, e_w_up, e_w_down, s_w_gate, s_w_up, s_w_down, final_g):
    b, s, d = x.shape
    depth = ada_w.shape[0]
    n_a = a_w_qkv.shape[0]
    a_heads = a_w_qkv.shape[2] // (3 * 2 * A_HEAD_DIM)
    b_heads = rel_bias.shape[1]
    b_head_dim = b_w_q.shape[2] // b_heads
    assert 2 * a_heads == b_heads and 1 <= n_a < depth and b_head_dim == A_HEAD_DIM

    c_pad = jnp.pad(c, ((0, 8 - b), (0, 0)))

    def ada(w, layer, bias, parts):
        return _ada_proj(c_pad, w, layer, bias)[:b].reshape(b, parts, 1, d)

    bias_tiles = _bias_tiles(rel_bias, MOBA_BLOCK, s, A_HEAD_DIM ** -0.5)

    def vec(m, k):
        return m[:, k]

    mod = ada(ada_w, 0, ada_b[0], 6)
    (h,) = _resmod(x, [], None, [(vec(mod, 0), vec(mod, 1), True)], False, [BF16])
    kv = None
    for layer in range(depth):
        if layer < n_a:
            lambda_init = 0.8 - 0.6 * math.exp(-0.3 * layer)
            qkv = _matmul(h.reshape(b * s, d), a_w_qkv, layer, BF16)
            o = _diff_attention(qkv.reshape(b, s, -1), a_lambda[layer], a_subln_g[layer], bias_tiles,
                                a_heads, lambda_init)
            w_o, w_o_layer = a_w_o, layer
        else:
            j = layer - n_a
            hq = h
            if kv is None:
                hq, hkv = h
                kv = _matmul(hkv.reshape(b * s, d), kv_w[None], 0, BF16).reshape(b, s, -1)
            q = _matmul(hq.reshape(b * s, d), b_w_q, j, BF16).reshape(b, s, -1)
            o = _moba_attention(q, kv, bias_tiles, b_heads)
            w_o, w_o_layer = b_w_o, j
        mix = _matmul(o.reshape(b * s, -1), w_o, w_o_layer, F32).reshape(b, s, d)
        ffn_mod = (vec(mod, 3), vec(mod, 4), True)
        x, h, h32 = _resmod(x, [mix], vec(mod, 2), [ffn_mod, ffn_mod], True, [BF16, F32])
        routed, shared = _moe(h, h32, layer, router_w, router_b, e_w_gate, e_w_up, e_w_down,
                              s_w_gate, s_w_up, s_w_down)
        g_ffn = vec(mod, 5)
        if layer + 1 == depth:
            zero = jnp.zeros((1, 1, d), F32)
            (out,) = _resmod(x, [routed, shared], g_ffn, [(zero, final_g.reshape(1, 1, d), False)], False, [F32])
            return out
        mod = ada(ada_w, layer + 1, ada_b[layer + 1], 6)
        mods = [(vec(mod, 0), vec(mod, 1), True)]
        if layer + 1 == n_a:
            kvmod = ada(kv_ada_w[None], 0, kv_ada_b, 2)
            mods.append((vec(kvmod, 0), vec(kvmod, 1), True))
            x, hq, hkv = _resmod(x, [routed, shared], g_ffn, mods, True, [BF16, BF16])
            h = (hq, hkv)
        else:
            x, h = _resmod(x, [routed, shared], g_ffn, mods, True, [BF16])
```

```python
import functools
import math

import numpy as np
import jax
import jax.numpy as jnp
from jax import lax
from jax.experimental import pallas as pl
from jax.experimental.pallas import tpu as pltpu

F32 = jnp.float32
BF16 = jnp.bfloat16

A_HEAD_DIM = 128
MOBA_BLOCK = 256
MOBA_TOPK = 3
REL_BUCKETS = 32
REL_MAX_DIST = 128
MOE_TOP_K = 8
ROUTED_SCALE = 2.5
NORM_EPS = 1e-6
SUBLN_EPS = 1e-5

LANES = 128
SUBLANES = 8
VMEM_LIMIT = 56 * 1024 * 1024
LOG2E = math.log2(math.e)


def _cparams(*sem):
    return pltpu.CompilerParams(dimension_semantics=sem, vmem_limit_bytes=VMEM_LIMIT)


def _tile(n, pref):
    t = min(pref, n)
    while n % t:
        t //= 2
    return t


def _dot(a, b):
    return jnp.dot(a, b, preferred_element_type=F32)


def _dot_t(a, b):
    return lax.dot_general(a, b, (((1,), (1,)), ((), ())), preferred_element_type=F32)


def _pack_halves(v):
    half = v.shape[1] // 2
    lo = lax.bitcast_convert_type(v[:, :half].astype(BF16).astype(F32), jnp.uint32)
    hi = lax.bitcast_convert_type(v[:, half:].astype(BF16).astype(F32), jnp.uint32)
    return (lo >> 16) | (hi & jnp.uint32(0xFFFF0000))


def _unpack_halves(p):
    lo = lax.bitcast_convert_type(p << 16, F32)
    hi = lax.bitcast_convert_type(p & jnp.uint32(0xFFFF0000), F32)
    return lo, hi


def _ada_kernel(c_ref, w_ref, b_ref, o_ref):
    c = c_ref[...]
    s = (c * jax.nn.sigmoid(c)).astype(BF16)
    o_ref[...] = _dot(s, w_ref[...].astype(BF16)) + b_ref[...]


def _ada_proj(c_pad, w, layer, b):
    _, d, n = w.shape
    tn = _tile(n, 512)
    return pl.pallas_call(
        _ada_kernel,
        grid=(n // tn,),
        in_specs=[pl.BlockSpec((8, d), lambda j: (0, 0)),
                  pl.BlockSpec((None, d, tn), lambda j: (layer, 0, j)),
                  pl.BlockSpec((1, tn), lambda j: (0, j))],
        out_specs=pl.BlockSpec((8, tn), lambda j: (0, j)),
        out_shape=jax.ShapeDtypeStruct((8, n), F32),
        compiler_params=_cparams("parallel"),
        name="ada_proj",
    )(c_pad, w, b.reshape(1, n))


def _resmod_kernel(*refs, n_add, n_mod, emit_x, add_one):
    x_ref = refs[0]
    add_refs = refs[1:1 + n_add]
    pos = 1 + n_add
    x = x_ref[0]
    if n_add:
        g = refs[pos][0]
        pos += 1
        tot = add_refs[0][0].astype(F32)
        for r in add_refs[1:]:
            tot = tot + r[0].astype(F32)
        x = x + g * tot
    mod_refs = refs[pos:pos + 2 * n_mod]
    out_refs = refs[pos + 2 * n_mod:]
    oi = 0
    if emit_x:
        out_refs[0][0] = x
        oi = 1
    y = x * lax.rsqrt(jnp.mean(x * x, axis=-1, keepdims=True) + NORM_EPS)
    for k in range(n_mod):
        sh = mod_refs[2 * k][0]
        sc = mod_refs[2 * k + 1][0]
        if add_one[k]:
            sc = 1.0 + sc
        val = y * sc + sh
        if out_refs[oi + k].dtype == jnp.uint32:
            out_refs[oi + k][0] = _pack_halves(val)
        else:
            out_refs[oi + k][0] = val.astype(out_refs[oi + k].dtype)


def _resmod(x, addends, gate, mods, emit_x, out_dtypes):
    b, s, d = x.shape
    ts = _tile(s, 128)
    row = pl.BlockSpec((1, ts, d), lambda i, j: (i, j, 0))

    def vec_spec(v):
        if v.shape[0] == 1:
            return pl.BlockSpec((1, 1, d), lambda i, j: (0, 0, 0))
        return pl.BlockSpec((1, 1, d), lambda i, j: (i, 0, 0))

    args = [x] + list(addends)
    specs = [row] * (1 + len(addends))
    if addends:
        args.append(gate)
        specs.append(vec_spec(gate))
    for sh, sc, _ in mods:
        args += [sh, sc]
        specs += [vec_spec(sh), vec_spec(sc)]
    out_shape, out_specs = [], []
    if emit_x:
        out_shape.append(jax.ShapeDtypeStruct((b, s, d), F32))
        out_specs.append(row)
    for dt in out_dtypes:
        if dt == jnp.uint32:
            out_shape.append(jax.ShapeDtypeStruct((b, s, d // 2), dt))
            out_specs.append(pl.BlockSpec((1, ts, d // 2), lambda i, j: (i, j, 0)))
        else:
            out_shape.append(jax.ShapeDtypeStruct((b, s, d), dt))
            out_specs.append(row)
    kern = functools.partial(_resmod_kernel, n_add=len(addends), n_mod=len(mods), emit_x=emit_x,
                             add_one=tuple(m[2] for m in mods))
    return pl.pallas_call(
        kern, grid=(b, s // ts), in_specs=specs, out_specs=out_specs, out_shape=out_shape,
        compiler_params=_cparams("parallel", "parallel"), name="resmod",
    )(*args)


def _mm_kernel(a_ref, w_ref, o_ref, wb_sc):
    @pl.when(pl.program_id(1) == 0)
    def _():
        wb_sc[...] = w_ref[...].astype(wb_sc.dtype)

    o_ref[...] = _dot(a_ref[...], wb_sc[...]).astype(o_ref.dtype)


def _matmul(a, w, layer, out_dtype):
    m, k = a.shape
    n = w.shape[2]
    tm, tn = _tile(m, 1024), _tile(n, 512)
    return pl.pallas_call(
        _mm_kernel,
        grid=(n // tn, m // tm),
        in_specs=[pl.BlockSpec((tm, k), lambda j, i: (i, 0)),
                  pl.BlockSpec((None, k, tn), lambda j, i: (layer, 0, j))],
        out_specs=pl.BlockSpec((tm, tn), lambda j, i: (i, j)),
        out_shape=jax.ShapeDtypeStruct((m, n), out_dtype),
        scratch_shapes=[pltpu.VMEM((k, tn), BF16)],
        compiler_params=_cparams("parallel", "arbitrary"),
        name="proj",
    )(a, w)


def _t5_bucket_np(dist):
    n = np.maximum(dist, 0)
    max_exact = REL_BUCKETS // 2
    nf = np.maximum(n, 1).astype(np.float32)
    large = max_exact + (np.log(nf / np.float32(max_exact)) / np.float32(math.log(REL_MAX_DIST / max_exact))
                         * np.float32(REL_BUCKETS - max_exact)).astype(np.int32)
    large = np.minimum(large, REL_BUCKETS - 1)
    return np.where(n < max_exact, n, large).astype(np.int32)


def _bias_tile_kernel(rel_ref, bk_ref, o_ref, *, far_bucket, inv_scale):
    c = pl.program_id(0)
    bk = bk_ref[...]
    acc = jnp.zeros(bk.shape, F32)
    for bidx in range(REL_BUCKETS):
        acc = jnp.where(bk == bidx, rel_ref[bidx, c], acc)
    o_ref[0] = (acc - rel_ref[far_bucket, c]) * inv_scale


def _bias_tiles(rel_bias, t, seq, scale):
    i = np.arange(t)[:, None]
    j = np.arange(t)[None, :]
    bk = np.stack([_t5_bucket_np(i - j), _t5_bucket_np(i - j + t)])
    far = _t5_bucket_np(np.arange(t + 1, max(seq, t + 2)))
    assert (far == far[0]).all(), "distances beyond two tiles must share one bucket"
    ncol = rel_bias.shape[1]
    kern = functools.partial(_bias_tile_kernel, far_bucket=int(far[0]), inv_scale=1.0 / scale)
    return pl.pallas_call(
        kern,
        grid=(ncol,),
        in_specs=[pl.BlockSpec(memory_space=pltpu.SMEM),
                  pl.BlockSpec((2, t, t), lambda c: (0, 0, 0))],
        out_specs=pl.BlockSpec((1, 2, t, t), lambda c: (c, 0, 0, 0)),
        out_shape=jax.ShapeDtypeStruct((ncol, 2, t, t), F32),
        compiler_params=_cparams("parallel"),
        name="bias_tiles",
    )(rel_bias, jnp.asarray(bk))


def _lane_fold(x, op, part):
    for c in range(x.shape[1] // LANES):
        piece = x[:, c * LANES:(c + 1) * LANES]
        part = piece if part is None else op(part, piece)
    return part


def _softmax_rows(s_sc, p_sc, n_cols, c2):
    tile = s_sc.shape[0]
    mpart = None
    for c0 in range(0, n_cols, tile):
        mpart = _lane_fold(s_sc[:, c0:c0 + tile], jnp.maximum, mpart)
    m = jnp.max(mpart, axis=1, keepdims=True)
    lpart = None
    for c0 in range(0, n_cols, tile):
        p = jnp.exp2((s_sc[:, c0:c0 + tile] - m) * c2)
        p_sc[:, c0:c0 + tile] = p.astype(p_sc.dtype)
        lpart = _lane_fold(p, jnp.add, lpart)
    return jnp.sum(lpart, axis=1, keepdims=True)


def _diff_attn_kernel(lam_ref, g_ref, q_ref, k_ref, v_ref, bias_ref, o_ref, s0, s1, p0, p1, *, t, lambda_init):
    seq = q_ref.shape[1]
    dh = A_HEAD_DIM
    c2 = dh ** -0.5 * LOG2E
    lv = lam_ref[...]
    lam = (jnp.exp(jnp.sum(lv[0:1] * lv[1:2], axis=1, keepdims=True))
           - jnp.exp(jnp.sum(lv[2:3] * lv[3:4], axis=1, keepdims=True)) + lambda_init)
    causal = (lax.broadcasted_iota(jnp.int32, (t, t), 0) >= lax.broadcasted_iota(jnp.int32, (t, t), 1))
    for qi in range(seq // t):
        r0 = qi * t
        n_cols = r0 + t
        heads = []
        for mp, (s_sc, p_sc) in enumerate(((s0, p0), (s1, p1))):
            q = q_ref[0, r0:r0 + t, mp * dh:(mp + 1) * dh]
            for j in range(qi + 1):
                s = _dot_t(q, k_ref[0, j * t:(j + 1) * t, mp * dh:(mp + 1) * dh])
                if j == qi:
                    s = jnp.where(causal, s + bias_ref[0, mp, 0], -jnp.inf)
                elif j == qi - 1:
                    s = s + bias_ref[0, mp, 1]
                s_sc[:, j * t:(j + 1) * t] = s
            l = _softmax_rows(s_sc, p_sc, n_cols, c2)
            heads.append(_dot(p_sc[:, :n_cols], v_ref[0, :n_cols, :]) / l)
        o = heads[0] - lam * heads[1]
        o = o * lax.rsqrt(jnp.mean(o * o, axis=-1, keepdims=True) + SUBLN_EPS)
        o_ref[0, r0:r0 + t, :] = (o * g_ref[...] * (1.0 - lambda_init)).astype(o_ref.dtype)


def _diff_attention(qkv, lam_vecs, subln_g, bias_tiles, n_heads, lambda_init):
    b, s, w3 = qkv.shape
    w = w3 // 3
    hw = 2 * A_HEAD_DIM
    t = bias_tiles.shape[-1]
    bt = bias_tiles.reshape(n_heads, 2, 2, t, t)
    kern = functools.partial(_diff_attn_kernel, t=t, lambda_init=lambda_init)
    return pl.pallas_call(
        kern,
        grid=(b, n_heads),
        in_specs=[pl.BlockSpec((4, A_HEAD_DIM), lambda bi, h: (0, 0)),
                  pl.BlockSpec((1, hw), lambda bi, h: (0, 0)),
                  pl.BlockSpec((1, s, hw), lambda bi, h: (bi, 0, h)),
                  pl.BlockSpec((1, s, hw), lambda bi, h: (bi, 0, n_heads + h)),
                  pl.BlockSpec((1, s, hw), lambda bi, h: (bi, 0, 2 * n_heads + h)),
                  pl.BlockSpec((1, 2, 2, t, t), lambda bi, h: (h, 0, 0, 0, 0))],
        out_specs=pl.BlockSpec((1, s, hw), lambda bi, h: (bi, 0, h)),
        out_shape=jax.ShapeDtypeStruct((b, s, w), BF16),
        scratch_shapes=[pltpu.VMEM((t, s), F32), pltpu.VMEM((t, s), F32),
                        pltpu.VMEM((t, s), BF16), pltpu.VMEM((t, s), BF16)],
        compiler_params=_cparams("parallel", "parallel"),
        name="diff_attn",
    )(lam_vecs, subln_g.reshape(1, hw), qkv, qkv, qkv, bt)


def _moba_kernel(q_ref, k_ref, v_ref, bias_ref, o_ref, s_sc, p_sc, mask_sc, *, t):
    seq = q_ref.shape[1]
    dh = q_ref.shape[2]
    nb = seq // t
    c2 = dh ** -0.5 * LOG2E
    causal = (lax.broadcasted_iota(jnp.int32, (t, t), 0) >= lax.broadcasted_iota(jnp.int32, (t, t), 1))
    r = lax.broadcasted_iota(jnp.int32, (LANES, seq), 0)
    c = lax.broadcasted_iota(jnp.int32, (LANES, seq), 1)
    ind = jnp.where(c >= r * t, jnp.where(c < (r + 1) * t, 1.0, 0.0), 0.0).astype(k_ref.dtype)
    kmean = (_dot(ind, k_ref[0]) * (1.0 / t)).astype(k_ref.dtype)
    lane = lax.broadcasted_iota(jnp.int32, (t, LANES), 1)
    for qi in range(nb):
        r0 = qi * t
        n_cols = r0 + t
        q = q_ref[0, r0:r0 + t, :]
        masked = qi > MOBA_TOPK
        if masked:
            gate = jnp.where(lane < qi, _dot_t(q, kmean), -jnp.inf)
            for n in range(qi):
                gn = gate[:, n:n + 1]
                ahead = jnp.where(gate > gn, 1.0, jnp.where(lane < n, jnp.where(gate == gn, 1.0, 0.0), 0.0))
                cnt = jnp.sum(ahead, axis=1, keepdims=True)
                mask_sc[n] = jnp.broadcast_to(jnp.where(cnt < MOBA_TOPK, 0.0, -jnp.inf), (t, LANES))
        for j in range(qi + 1):
            s = _dot_t(q, k_ref[0, j * t:(j + 1) * t, :])
            if j == qi:
                s = jnp.where(causal, s + bias_ref[0, 0], -jnp.inf)
            else:
                if j == qi - 1:
                    s = s + bias_ref[0, 1]
                if masked:
                    mk = mask_sc[j]
                    s = s + jnp.concatenate([mk] * (t // LANES), axis=1)
            s_sc[:, j * t:(j + 1) * t] = s
        l = _softmax_rows(s_sc, p_sc, n_cols, c2)
        o_ref[0, r0:r0 + t, :] = (_dot(p_sc[:, :n_cols], v_ref[0, :n_cols, :]) / l).astype(o_ref.dtype)


def _moba_attention(q, kv, bias_tiles, n_heads):
    b, s, w = q.shape
    dh = w // n_heads
    t = MOBA_BLOCK
    nb = s // t
    assert s % t == 0 and bias_tiles.shape[-1] == t and nb <= LANES
    kern = functools.partial(_moba_kernel, t=t)
    return pl.pallas_call(
        kern,
        grid=(b, n_heads),
        in_specs=[pl.BlockSpec((1, s, dh), lambda bi, h: (bi, 0, h)),
                  pl.BlockSpec((1, s, dh), lambda bi, h: (bi, 0, h)),
                  pl.BlockSpec((1, s, dh), lambda bi, h: (bi, 0, n_heads + h)),
                  pl.BlockSpec((1, 2, t, t), lambda bi, h: (h, 0, 0, 0))],
        out_specs=pl.BlockSpec((1, s, dh), lambda bi, h: (bi, 0, h)),
        out_shape=jax.ShapeDtypeStruct((b, s, w), BF16),
        scratch_shapes=[pltpu.VMEM((t, s), F32), pltpu.VMEM((t, s), BF16), pltpu.VMEM((nb, t, LANES), F32)],
        compiler_params=_cparams("parallel", "parallel"),
        name="moba_attn",
    )(q, kv, kv, bias_tiles)


def _top_experts(h_ref, w_ref, b_ref):
    aff = jax.nn.sigmoid(_dot(h_ref[...], w_ref[...]))
    work = aff + b_ref[...]
    lane = lax.broadcasted_iota(jnp.int32, aff.shape, 1).astype(F32)
    mask = jnp.zeros(aff.shape, F32)
    firsts, affs = [], []
    for _ in range(MOE_TOP_K):
        mx = jnp.max(work, axis=1, keepdims=True)
        first = jnp.min(jnp.where(work == mx, lane, float(LANES)), axis=1, keepdims=True)
        pick = lane == first
        affs.append(jnp.sum(jnp.where(pick, aff, 0.0), axis=1, keepdims=True))
        firsts.append(first)
        mask = jnp.where(pick, 1.0, mask)
        work = jnp.where(pick, -jnp.inf, work)
    return lane, mask, firsts, affs


def _router_count_kernel(h_ref, w_ref, b_ref, cnt_ref):
    @pl.when(pl.program_id(0) == 0)
    def _():
        cnt_ref[...] = jnp.zeros_like(cnt_ref)

    _, mask, _, _ = _top_experts(h_ref, w_ref, b_ref)
    cnt_ref[...] += jnp.broadcast_to(jnp.sum(mask, axis=0, keepdims=True), cnt_ref.shape)


def _router_kernel(h_ref, w_ref, b_ref, st_ref, cst_ref, g_ref, pos_ref, cpos_ref, carry_sc):
    @pl.when(pl.program_id(0) == 0)
    def _():
        carry_sc[...] = jnp.zeros_like(carry_sc)

    lane, mask, firsts, affs = _top_experts(h_ref, w_ref, b_ref)
    tm = mask.shape[0]
    denom = affs[0]
    for a in affs[1:]:
        denom = denom + a
    tri = jnp.where(lax.broadcasted_iota(jnp.int32, (tm, tm), 0) > lax.broadcasted_iota(jnp.int32, (tm, tm), 1),
                    1.0, 0.0).astype(BF16)
    carry = carry_sc[...]
    rank = _dot(tri, mask.astype(BF16)) + carry
    carry_sc[...] = carry + jnp.sum(mask, axis=0, keepdims=True)
    row = rank + st_ref[...]
    crow = rank + cst_ref[...]
    g_out = jnp.zeros(mask.shape, F32)
    p_out = jnp.zeros(mask.shape, F32)
    c_out = jnp.zeros(mask.shape, F32)
    for k in range(MOE_TOP_K):
        slot = lane == float(k)
        mine = lane == firsts[k]
        g_out = jnp.where(slot, affs[k] / denom * ROUTED_SCALE, g_out)
        p_out = jnp.where(slot, jnp.sum(jnp.where(mine, row, 0.0), axis=1, keepdims=True), p_out)
        c_out = jnp.where(slot, jnp.sum(jnp.where(mine, crow, 0.0), axis=1, keepdims=True), c_out)
    g_ref[...] = g_out
    pos_ref[...] = p_out.astype(jnp.int32)
    cpos_ref[...] = c_out.astype(jnp.int32)


def _route(h2d, router_w, router_b, tm_rows):
    t, d = h2d.shape
    n_exp = router_w.shape[1]
    assert n_exp <= LANES and MOE_TOP_K <= LANES and t * MOE_TOP_K + n_exp * tm_rows < 2 ** 24
    w = jnp.pad(router_w, ((0, 0), (0, LANES - n_exp))).astype(BF16)
    bias = jnp.pad(router_b, (0, LANES - n_exp), constant_values=-jnp.inf).reshape(1, LANES)
    tm = _tile(t, 256)
    tok = pl.BlockSpec((tm, LANES), lambda i: (i, 0))
    vec = pl.BlockSpec((1, LANES), lambda i: (0, 0))
    in_specs = [pl.BlockSpec((tm, d), lambda i: (i, 0)), pl.BlockSpec((d, LANES), lambda i: (0, 0)), vec]
    cnt = pl.pallas_call(
        _router_count_kernel,
        grid=(t // tm,),
        in_specs=in_specs,
        out_specs=pl.BlockSpec((SUBLANES, LANES), lambda i: (0, 0)),
        out_shape=jax.ShapeDtypeStruct((SUBLANES, LANES), F32),
        compiler_params=_cparams("arbitrary"),
        name="router_count",
    )(h2d, w, bias)
    counts = cnt[0, :n_exp].astype(jnp.int32)
    padded = (counts + tm_rows - 1) // tm_rows * tm_rows
    ends = jnp.cumsum(padded)
    starts = ends - padded
    cstarts = jnp.cumsum(counts) - counts

    def lanes(v):
        return jnp.pad(v.astype(F32), (0, LANES - n_exp)).reshape(1, LANES)

    gates, pos, cpos = pl.pallas_call(
        _router_kernel,
        grid=(t // tm,),
        in_specs=in_specs + [vec, vec],
        out_specs=[tok, tok, tok],
        out_shape=[jax.ShapeDtypeStruct((t, LANES), F32), jax.ShapeDtypeStruct((t, LANES), jnp.int32),
                   jax.ShapeDtypeStruct((t, LANES), jnp.int32)],
        scratch_shapes=[pltpu.VMEM((1, LANES), F32)],
        compiler_params=_cparams("arbitrary"),
        name="router",
    )(h2d, w, bias, lanes(starts), lanes(cstarts))
    pos = pos[:, :MOE_TOP_K].reshape(-1)
    token_of = jnp.arange(t * MOE_TOP_K, dtype=jnp.int32) // MOE_TOP_K
    _, sorted_tok = lax.sort_key_val(cpos[:, :MOE_TOP_K].reshape(-1), token_of)
    sorted_tok = jnp.concatenate([sorted_tok, jnp.zeros((tm_rows,), jnp.int32)])
    n_tiles = t * MOE_TOP_K // tm_rows + n_exp
    row0 = jnp.arange(n_tiles, dtype=jnp.int32) * tm_rows
    tile_expert = jnp.minimum(jnp.sum(ends[None, :] <= row0[:, None], axis=1), n_exp - 1).astype(jnp.int32)
    n_used = (ends[-1:] // tm_rows).astype(jnp.int32)
    src0 = cstarts[tile_expert] + row0 - starts[tile_expert]
    src0 = jnp.where(row0 < ends[-1], src0, 0).astype(jnp.int32)
    return gates, pos, sorted_tok, tile_expert, n_used, src0


def _experts_kernel(te_ref, nu_ref, src_ref, tok_ref, h_ref, wg_ref, wu_ref, wd_ref, y_ref,
                    xbuf, xb_sc, wgu_sc, wd_sc, sems):
    i = pl.program_id(0)
    tm = xbuf.shape[1]
    f = wg_ref.shape[1]
    n_used = nu_ref[0]
    live = i < n_used
    has_next = i + 1 < n_used
    slot = i % 2

    def row_copy(slot_, r, tok):
        return pltpu.make_async_copy(h_ref.at[pl.ds(tok, 1)], xbuf.at[slot_, pl.ds(r, 1)], sems.at[slot_])

    def issue(tile, slot_):
        s0 = src_ref[tile]
        for r in range(tm):
            row_copy(slot_, r, tok_ref[s0 + r]).start()

    def drain(slot_):
        for r in range(tm):
            row_copy(slot_, r, 0).wait()

    def stage():
        lo, hi = _unpack_halves(xbuf[slot])
        half = lo.shape[1]
        xb_sc[:, :half] = lo.astype(xb_sc.dtype)
        xb_sc[:, half:] = hi.astype(xb_sc.dtype)

    def compute():
        hgu = _dot(xb_sc[...], wgu_sc[...])
        hg = hgu[:, :f]
        hid = (hg * jax.nn.sigmoid(hg) * hgu[:, f:]).astype(wd_sc.dtype)
        y_ref[...] = _pack_halves(_dot(hid, wd_sc[...]))

    @pl.when(i == 0)
    def _():
        issue(0, 0)

    fresh = jnp.logical_or(i == 0, te_ref[i] != te_ref[jnp.maximum(i - 1, 0)])

    @pl.when(jnp.logical_and(live, fresh))
    def _():
        wgu_sc[:, :f] = wg_ref[...].astype(wgu_sc.dtype)
        wgu_sc[:, f:] = wu_ref[...].astype(wgu_sc.dtype)
        wd_sc[...] = wd_ref[...].astype(wd_sc.dtype)

    @pl.when(jnp.logical_and(live, has_next))
    def _():
        drain(slot)
        stage()
        issue(i + 1, 1 - slot)
        compute()

    @pl.when(jnp.logical_and(live, jnp.logical_not(has_next)))
    def _():
        drain(slot)
        stage()
        compute()

    @pl.when(jnp.logical_not(live))
    def _():
        y_ref[...] = jnp.zeros_like(y_ref)


def _experts(h2d, sorted_tok, tile_expert, n_used, src0, wg, wu, wd, layer, tm):
    t, dp = h2d.shape
    d = 2 * dp
    f = wg.shape[3]
    n_tiles = tile_expert.shape[0]
    return pl.pallas_call(
        _experts_kernel,
        grid_spec=pltpu.PrefetchScalarGridSpec(
            num_scalar_prefetch=4,
            grid=(n_tiles,),
            in_specs=[pl.BlockSpec(memory_space=pl.ANY),
                      pl.BlockSpec((None, None, d, f), lambda i, te, *_: (layer, te[i], 0, 0)),
                      pl.BlockSpec((None, None, d, f), lambda i, te, *_: (layer, te[i], 0, 0)),
                      pl.BlockSpec((None, None, f, d), lambda i, te, *_: (layer, te[i], 0, 0))],
            out_specs=pl.BlockSpec((tm, dp), lambda i, *_: (i, 0)),
            scratch_shapes=[pltpu.VMEM((2, tm, dp), jnp.uint32), pltpu.VMEM((tm, d), BF16),
                            pltpu.VMEM((d, 2 * f), BF16), pltpu.VMEM((f, d), BF16),
                            pltpu.SemaphoreType.DMA((2,))]),
        out_shape=jax.ShapeDtypeStruct((n_tiles * tm, dp), jnp.uint32),
        compiler_params=_cparams("arbitrary"),
        name="experts",
    )(tile_expert, n_used, src0, sorted_tok, h2d, wg, wu, wd)


def _combine_kernel(pos_ref, g_ref, y_ref, o_ref, buf, sems, *, tt):
    i = pl.program_id(0)
    slot = i % 2

    def row_copy(slot_, t, k, p):
        return pltpu.make_async_copy(y_ref.at[pl.ds(p, 1)], buf.at[slot_, k, pl.ds(t, 1)], sems.at[slot_])

    def issue(tile, slot_):
        base = tile * (tt * MOE_TOP_K)

        def body(t, c):
            for k in range(MOE_TOP_K):
                row_copy(slot_, t, k, pos_ref[base + t * MOE_TOP_K + k]).start()
            return c

        lax.fori_loop(0, tt, body, 0)

    def drain(slot_):
        def body(t, c):
            for k in range(MOE_TOP_K):
                row_copy(slot_, 0, 0, 0).wait()
            return c

        lax.fori_loop(0, tt, body, 0)

    @pl.when(i == 0)
    def _():
        issue(0, 0)

    @pl.when(i + 1 < pl.num_programs(0))
    def _():
        issue(i + 1, 1 - slot)

    drain(slot)
    g = g_ref[...]
    half = buf.shape[-1]
    acc_lo = acc_hi = None
    for k in range(MOE_TOP_K):
        lo, hi = _unpack_halves(buf[slot, k])
        gk = g[:, k:k + 1]
        acc_lo = gk * lo if acc_lo is None else acc_lo + gk * lo
        acc_hi = gk * hi if acc_hi is None else acc_hi + gk * hi
    o_ref[:, :half] = acc_lo
    o_ref[:, half:] = acc_hi


def _combine(y, pos, gates):
    t = gates.shape[0]
    dp = y.shape[1]
    d = 2 * dp
    tt = _tile(t, 64)
    kern = functools.partial(_combine_kernel, tt=tt)
    return pl.pallas_call(
        kern,
        grid_spec=pltpu.PrefetchScalarGridSpec(
            num_scalar_prefetch=1,
            grid=(t // tt,),
            in_specs=[pl.BlockSpec((tt, LANES), lambda i, *_: (i, 0)),
                      pl.BlockSpec(memory_space=pl.ANY)],
            out_specs=pl.BlockSpec((tt, d), lambda i, *_: (i, 0)),
            scratch_shapes=[pltpu.VMEM((2, MOE_TOP_K, tt, dp), jnp.uint32), pltpu.SemaphoreType.DMA((2,))]),
        out_shape=jax.ShapeDtypeStruct((t, d), F32),
        compiler_params=_cparams("arbitrary"),
        name="moe_combine",
    )(pos, gates, y)


def _shared_kernel(x_ref, wg_ref, wu_ref, wd_ref, o_ref, wgu_sc, wd_sc):
    f = wg_ref.shape[1]

    @pl.when(pl.program_id(0) == 0)
    def _():
        wgu_sc[:, :f] = wg_ref[...].astype(wgu_sc.dtype)
        wgu_sc[:, f:] = wu_ref[...].astype(wgu_sc.dtype)
        wd_sc[...] = wd_ref[...].astype(wd_sc.dtype)

    hgu = _dot(x_ref[...], wgu_sc[...])
    hg = hgu[:, :f]
    hid = (hg * jax.nn.sigmoid(hg) * hgu[:, f:]).astype(wd_sc.dtype)
    o_ref[...] = _dot(hid, wd_sc[...])


def _shared_expert(h2d, wg, wu, wd, layer):
    t, d = h2d.shape
    f = wg.shape[2]
    tm = _tile(t, 512)
    return pl.pallas_call(
        _shared_kernel,
        grid=(t // tm,),
        in_specs=[pl.BlockSpec((tm, d), lambda i: (i, 0)),
                  pl.BlockSpec((None, d, f), lambda i: (layer, 0, 0)),
                  pl.BlockSpec((None, d, f), lambda i: (layer, 0, 0)),
                  pl.BlockSpec((None, f, d), lambda i: (layer, 0, 0))],
        out_specs=pl.BlockSpec((tm, d), lambda i: (i, 0)),
        out_shape=jax.ShapeDtypeStruct((t, d), F32),
        scratch_shapes=[pltpu.VMEM((d, 2 * f), BF16), pltpu.VMEM((f, d), BF16)],
        compiler_params=_cparams("arbitrary"),
        name="shared_expert",
    )(h2d, wg, wu, wd)


EXPERT_ROW_TILE = 256


def _moe(h, h32, layer, router_w, router_b, wg, wu, wd, sg, su, sd):
    b, s, d = h.shape
    t = b * s
    tm = EXPERT_ROW_TILE
    h2d = h.reshape(t, d)
    gates, pos, sorted_tok, tile_expert, n_used, src0 = _route(h2d, router_w[layer], router_b[layer], tm)
    y = _experts(h32.reshape(t, d // 2), sorted_tok, tile_expert, n_used, src0, wg, wu, wd, layer, tm)
    routed = _combine(y, pos, gates)
    shared = _shared_expert(h2d, sg, su, sd, layer)
    return routed.reshape(b, s, d), shared.reshape(b, s, d)


def kernel(x, c, rel_bias, ada_w, ada_b, a_w_qkv, a_lambda, a_subln_g, a_w_o, kv_ada_w, kv_ada_b, kv_w,
           b_w_q, b_w_o, router_w, router_b, e_w_gate, e_w_up, e_w_down, s_w_gate, s_w_up, s_w_down, final_g):
    b, s, d = x.shape
    depth = ada_w.shape[0]
    n_a = a_w_qkv.shape[0]
    a_heads = a_w_qkv.shape[2] // (3 * 2 * A_HEAD_DIM)
    b_heads = rel_bias.shape[1]
    b_head_dim = b_w_q.shape[2] // b_heads
    assert 2 * a_heads == b_heads and 1 <= n_a < depth and b_head_dim == A_HEAD_DIM

    c_pad = jnp.pad(c, ((0, 8 - b), (0, 0)))

    def ada(w, layer, bias, parts):
        return _ada_proj(c_pad, w, layer, bias)[:b].reshape(b, parts, 1, d)

    bias_tiles = _bias_tiles(rel_bias, MOBA_BLOCK, s, A_HEAD_DIM ** -0.5)

    def vec(m, k):
        return m[:, k]

    mod = ada(ada_w, 0, ada_b[0], 6)
    (h,) = _resmod(x, [], None, [(vec(mod, 0), vec(mod, 1), True)], False, [BF16])
    kv = None
    for layer in range(depth):
        if layer < n_a:
            lambda_init = 0.8 - 0.6 * math.exp(-0.3 * layer)
            qkv = _matmul(h.reshape(b * s, d), a_w_qkv, layer, BF16)
            o = _diff_attention(qkv.reshape(b, s, -1), a_lambda[layer], a_subln_g[layer], bias_tiles,
                                a_heads, lambda_init)
            w_o, w_o_layer = a_w_o, layer
        else:
            j = layer - n_a
            hq = h
            if kv is None:
                hq, hkv = h
                kv = _matmul(hkv.reshape(b * s, d), kv_w[None], 0, BF16).reshape(b, s, -1)
            q = _matmul(hq.reshape(b * s, d), b_w_q, j, BF16).reshape(b, s, -1)
            o = _moba_attention(q, kv, bias_tiles, b_heads)
            w_o, w_o_layer = b_w_o, j
        mix = _matmul(o.reshape(b * s, -1), w_o, w_o_layer, F32).reshape(b, s, d)
        ffn_mod = (vec(mod, 3), vec(mod, 4), True)
        x, h, h32 = _resmod(x, [mix], vec(mod, 2), [ffn_mod, ffn_mod], True, [BF16, jnp.uint32])
        routed, shared = _moe(h, h32, layer, router_w, router_b, e_w_gate, e_w_up, e_w_down,
                              s_w_gate, s_w_up, s_w_down)
        g_ffn = vec(mod, 5)
        if layer + 1 == depth:
            zero = jnp.zeros((1, 1, d), F32)
            (out,) = _resmod(x, [routed, shared], g_ffn, [(zero, final_g.reshape(1, 1, d), False)], False, [F32])
            return out
        mod = ada(ada_w, layer + 1, ada_b[layer + 1], 6)
        mods = [(vec(mod, 0), vec(mod, 1), True)]
        if layer + 1 == n_a:
            kvmod = ada(kv_ada_w[None], 0, kv_ada_b, 2)
            mods.append((vec(kvmod, 0), vec(kvmod, 1), True))
            x, hq, hkv = _resmod(x, [routed, shared], g_ffn, mods, True, [BF16, BF16])
            h = (hq, hkv)
        else:
            x, h = _resmod(x, [routed, shared], g_ffn, mods, True, [BF16])
```

```python
import functools
import math

import numpy as np
import jax
import jax.numpy as jnp
from jax import lax
from jax.experimental import pallas as pl
from jax.experimental.pallas import tpu as pltpu

F32 = jnp.float32
BF16 = jnp.bfloat16

A_HEAD_DIM = 128
MOBA_BLOCK = 256
MOBA_TOPK = 3
REL_BUCKETS = 32
REL_MAX_DIST = 128
MOE_TOP_K = 8
ROUTED_SCALE = 2.5
NORM_EPS = 1e-6
SUBLN_EPS = 1e-5

LANES = 128
SUBLANES = 8
VMEM_LIMIT = 56 * 1024 * 1024
LOG2E = math.log2(math.e)


def _cparams(*sem):
    return pltpu.CompilerParams(dimension_semantics=sem, vmem_limit_bytes=VMEM_LIMIT)


def _tile(n, pref):
    t = min(pref, n)
    while n % t:
        t //= 2
    return t


def _dot(a, b):
    return jnp.dot(a, b, preferred_element_type=F32)


def _dot_t(a, b):
    return lax.dot_general(a, b, (((1,), (1,)), ((), ())), preferred_element_type=F32)


def _pack_halves(v):
    half = v.shape[1] // 2
    lo = lax.bitcast_convert_type(v[:, :half].astype(BF16).astype(F32), jnp.uint32)
    hi = lax.bitcast_convert_type(v[:, half:].astype(BF16).astype(F32), jnp.uint32)
    return (lo >> 16) | (hi & jnp.uint32(0xFFFF0000))


def _unpack_halves(p):
    lo = lax.bitcast_convert_type(p << 16, F32)
    hi = lax.bitcast_convert_type(p & jnp.uint32(0xFFFF0000), F32)
    return lo, hi


def _ada_kernel(c_ref, w_ref, b_ref, o_ref):
    c = c_ref[...]
    s = (c * jax.nn.sigmoid(c)).astype(BF16)
    o_ref[...] = _dot(s, w_ref[...].astype(BF16)) + b_ref[...]


def _ada_proj(c_pad, w, layer, b):
    _, d, n = w.shape
    tn = _tile(n, 512)
    return pl.pallas_call(
        _ada_kernel,
        grid=(n // tn,),
        in_specs=[pl.BlockSpec((8, d), lambda j: (0, 0)),
                  pl.BlockSpec((None, d, tn), lambda j: (layer, 0, j)),
                  pl.BlockSpec((1, tn), lambda j: (0, j))],
        out_specs=pl.BlockSpec((8, tn), lambda j: (0, j)),
        out_shape=jax.ShapeDtypeStruct((8, n), F32),
        compiler_params=_cparams("parallel"),
        name="ada_proj",
    )(c_pad, w, b.reshape(1, n))


def _resmod_kernel(*refs, n_add, n_mod, emit_x, add_one):
    x_ref = refs[0]
    add_refs = refs[1:1 + n_add]
    pos = 1 + n_add
    x = x_ref[0]
    if n_add:
        g = refs[pos][0]
        pos += 1
        tot = add_refs[0][0].astype(F32)
        for r in add_refs[1:]:
            tot = tot + r[0].astype(F32)
        x = x + g * tot
    mod_refs = refs[pos:pos + 2 * n_mod]
    out_refs = refs[pos + 2 * n_mod:]
    oi = 0
    if emit_x:
        out_refs[0][0] = x
        oi = 1
    y = x * lax.rsqrt(jnp.mean(x * x, axis=-1, keepdims=True) + NORM_EPS)
    for k in range(n_mod):
        sh = mod_refs[2 * k][0]
        sc = mod_refs[2 * k + 1][0]
        if add_one[k]:
            sc = 1.0 + sc
        val = y * sc + sh
        if out_refs[oi + k].dtype == jnp.uint32:
            out_refs[oi + k][0] = _pack_halves(val)
        else:
            out_refs[oi + k][0] = val.astype(out_refs[oi + k].dtype)


def _resmod(x, addends, gate, mods, emit_x, out_dtypes):
    b, s, d = x.shape
    ts = _tile(s, 128)
    row = pl.BlockSpec((1, ts, d), lambda i, j: (i, j, 0))

    def vec_spec(v):
        if v.shape[0] == 1:
            return pl.BlockSpec((1, 1, d), lambda i, j: (0, 0, 0))
        return pl.BlockSpec((1, 1, d), lambda i, j: (i, 0, 0))

    args = [x] + list(addends)
    specs = [row] * (1 + len(addends))
    if addends:
        args.append(gate)
        specs.append(vec_spec(gate))
    for sh, sc, _ in mods:
        args += [sh, sc]
        specs += [vec_spec(sh), vec_spec(sc)]
    out_shape, out_specs = [], []
    if emit_x:
        out_shape.append(jax.ShapeDtypeStruct((b, s, d), F32))
        out_specs.append(row)
    for dt in out_dtypes:
        if dt == jnp.uint32:
            out_shape.append(jax.ShapeDtypeStruct((b, s, d // 2), dt))
            out_specs.append(pl.BlockSpec((1, ts, d // 2), lambda i, j: (i, j, 0)))
        else:
            out_shape.append(jax.ShapeDtypeStruct((b, s, d), dt))
            out_specs.append(row)
    kern = functools.partial(_resmod_kernel, n_add=len(addends), n_mod=len(mods), emit_x=emit_x,
                             add_one=tuple(m[2] for m in mods))
    return pl.pallas_call(
        kern, grid=(b, s // ts), in_specs=specs, out_specs=out_specs, out_shape=out_shape,
        compiler_params=_cparams("parallel", "parallel"), name="resmod",
    )(*args)


def _mm_kernel(a_ref, w_ref, o_ref, wb_sc):
    @pl.when(pl.program_id(1) == 0)
    def _():
        wb_sc[...] = w_ref[...].astype(wb_sc.dtype)

    o_ref[...] = _dot(a_ref[...], wb_sc[...]).astype(o_ref.dtype)


def _matmul(a, w, layer, out_dtype):
    m, k = a.shape
    n = w.shape[2]
    tm, tn = _tile(m, 1024), _tile(n, 512)
    return pl.pallas_call(
        _mm_kernel,
        grid=(n // tn, m // tm),
        in_specs=[pl.BlockSpec((tm, k), lambda j, i: (i, 0)),
                  pl.BlockSpec((None, k, tn), lambda j, i: (layer, 0, j))],
        out_specs=pl.BlockSpec((tm, tn), lambda j, i: (i, j)),
        out_shape=jax.ShapeDtypeStruct((m, n), out_dtype),
        scratch_shapes=[pltpu.VMEM((k, tn), BF16)],
        compiler_params=_cparams("parallel", "arbitrary"),
        name="proj",
    )(a, w)


def _t5_bucket_np(dist):
    n = np.maximum(dist, 0)
    max_exact = REL_BUCKETS // 2
    nf = np.maximum(n, 1).astype(np.float32)
    large = max_exact + (np.log(nf / np.float32(max_exact)) / np.float32(math.log(REL_MAX_DIST / max_exact))
                         * np.float32(REL_BUCKETS - max_exact)).astype(np.int32)
    large = np.minimum(large, REL_BUCKETS - 1)
    return np.where(n < max_exact, n, large).astype(np.int32)


def _bias_tile_kernel(rel_ref, bk_ref, o_ref, *, far_bucket, inv_scale):
    c = pl.program_id(0)
    bk = bk_ref[...]
    acc = jnp.zeros(bk.shape, F32)
    for bidx in range(REL_BUCKETS):
        acc = jnp.where(bk == bidx, rel_ref[bidx, c], acc)
    o_ref[0] = (acc - rel_ref[far_bucket, c]) * inv_scale


def _bias_tiles(rel_bias, t, seq, scale):
    i = np.arange(t)[:, None]
    j = np.arange(t)[None, :]
    bk = np.stack([_t5_bucket_np(i - j), _t5_bucket_np(i - j + t)])
    far = _t5_bucket_np(np.arange(t + 1, max(seq, t + 2)))
    assert (far == far[0]).all(), "distances beyond two tiles must share one bucket"
    ncol = rel_bias.shape[1]
    kern = functools.partial(_bias_tile_kernel, far_bucket=int(far[0]), inv_scale=1.0 / scale)
    return pl.pallas_call(
        kern,
        grid=(ncol,),
        in_specs=[pl.BlockSpec(memory_space=pltpu.SMEM),
                  pl.BlockSpec((2, t, t), lambda c: (0, 0, 0))],
        out_specs=pl.BlockSpec((1, 2, t, t), lambda c: (c, 0, 0, 0)),
        out_shape=jax.ShapeDtypeStruct((ncol, 2, t, t), F32),
        compiler_params=_cparams("parallel"),
        name="bias_tiles",
    )(rel_bias, jnp.asarray(bk))


def _lane_fold(x, op, part):
    for c in range(x.shape[1] // LANES):
        piece = x[:, c * LANES:(c + 1) * LANES]
        part = piece if part is None else op(part, piece)
    return part


def _softmax_rows(s_sc, p_sc, n_cols, c2):
    tile = s_sc.shape[0]
    mpart = None
    for c0 in range(0, n_cols, tile):
        mpart = _lane_fold(s_sc[:, c0:c0 + tile], jnp.maximum, mpart)
    m = jnp.max(mpart, axis=1, keepdims=True)
    lpart = None
    for c0 in range(0, n_cols, tile):
        p = jnp.exp2((s_sc[:, c0:c0 + tile] - m) * c2)
        p_sc[:, c0:c0 + tile] = p.astype(p_sc.dtype)
        lpart = _lane_fold(p, jnp.add, lpart)
    return jnp.sum(lpart, axis=1, keepdims=True)


def _diff_attn_kernel(lam_ref, g_ref, q_ref, k_ref, v_ref, bias_ref, o_ref, s0, s1, p0, p1, *, t, lambda_init):
    seq = q_ref.shape[1]
    dh = A_HEAD_DIM
    c2 = dh ** -0.5 * LOG2E
    lv = lam_ref[...]
    lam = (jnp.exp(jnp.sum(lv[0:1] * lv[1:2], axis=1, keepdims=True))
           - jnp.exp(jnp.sum(lv[2:3] * lv[3:4], axis=1, keepdims=True)) + lambda_init)
    causal = (lax.broadcasted_iota(jnp.int32, (t, t), 0) >= lax.broadcasted_iota(jnp.int32, (t, t), 1))
    for qi in range(seq // t):
        r0 = qi * t
        n_cols = r0 + t
        heads = []
        for mp, (s_sc, p_sc) in enumerate(((s0, p0), (s1, p1))):
            q = q_ref[0, r0:r0 + t, mp * dh:(mp + 1) * dh]
            for j in range(qi + 1):
                s = _dot_t(q, k_ref[0, j * t:(j + 1) * t, mp * dh:(mp + 1) * dh])
                if j == qi:
                    s = jnp.where(causal, s + bias_ref[0, mp, 0], -jnp.inf)
                elif j == qi - 1:
                    s = s + bias_ref[0, mp, 1]
                s_sc[:, j * t:(j + 1) * t] = s
            l = _softmax_rows(s_sc, p_sc, n_cols, c2)
            heads.append(_dot(p_sc[:, :n_cols], v_ref[0, :n_cols, :]) / l)
        o = heads[0] - lam * heads[1]
        o = o * lax.rsqrt(jnp.mean(o * o, axis=-1, keepdims=True) + SUBLN_EPS)
        o_ref[0, r0:r0 + t, :] = (o * g_ref[...] * (1.0 - lambda_init)).astype(o_ref.dtype)


def _diff_attention(qkv, lam_vecs, subln_g, bias_tiles, n_heads, lambda_init):
    b, s, w3 = qkv.shape
    w = w3 // 3
    hw = 2 * A_HEAD_DIM
    t = bias_tiles.shape[-1]
    bt = bias_tiles.reshape(n_heads, 2, 2, t, t)
    kern = functools.partial(_diff_attn_kernel, t=t, lambda_init=lambda_init)
    return pl.pallas_call(
        kern,
        grid=(b, n_heads),
        in_specs=[pl.BlockSpec((4, A_HEAD_DIM), lambda bi, h: (0, 0)),
                  pl.BlockSpec((1, hw), lambda bi, h: (0, 0)),
                  pl.BlockSpec((1, s, hw), lambda bi, h: (bi, 0, h)),
                  pl.BlockSpec((1, s, hw), lambda bi, h: (bi, 0, n_heads + h)),
                  pl.BlockSpec((1, s, hw), lambda bi, h: (bi, 0, 2 * n_heads + h)),
                  pl.BlockSpec((1, 2, 2, t, t), lambda bi, h: (h, 0, 0, 0, 0))],
        out_specs=pl.BlockSpec((1, s, hw), lambda bi, h: (bi, 0, h)),
        out_shape=jax.ShapeDtypeStruct((b, s, w), BF16),
        scratch_shapes=[pltpu.VMEM((t, s), F32), pltpu.VMEM((t, s), F32),
                        pltpu.VMEM((t, s), BF16), pltpu.VMEM((t, s), BF16)],
        compiler_params=_cparams("parallel", "parallel"),
        name="diff_attn",
    )(lam_vecs, subln_g.reshape(1, hw), qkv, qkv, qkv, bt)


def _moba_kernel(q_ref, k_ref, v_ref, bias_ref, o_ref, s_sc, p_sc, mask_sc, *, t):
    seq = q_ref.shape[1]
    dh = q_ref.shape[2]
    nb = seq // t
    c2 = dh ** -0.5 * LOG2E
    causal = (lax.broadcasted_iota(jnp.int32, (t, t), 0) >= lax.broadcasted_iota(jnp.int32, (t, t), 1))
    r = lax.broadcasted_iota(jnp.int32, (LANES, seq), 0)
    c = lax.broadcasted_iota(jnp.int32, (LANES, seq), 1)
    ind = jnp.where(c >= r * t, jnp.where(c < (r + 1) * t, 1.0, 0.0), 0.0).astype(k_ref.dtype)
    kmean = (_dot(ind, k_ref[0]) * (1.0 / t)).astype(k_ref.dtype)
    lane = lax.broadcasted_iota(jnp.int32, (t, LANES), 1)
    for qi in range(nb):
        r0 = qi * t
        n_cols = r0 + t
        q = q_ref[0, r0:r0 + t, :]
        masked = qi > MOBA_TOPK
        if masked:
            gate = jnp.where(lane < qi, _dot_t(q, kmean), -jnp.inf)
            for n in range(qi):
                gn = gate[:, n:n + 1]
                ahead = jnp.where(gate > gn, 1.0, jnp.where(lane < n, jnp.where(gate == gn, 1.0, 0.0), 0.0))
                cnt = jnp.sum(ahead, axis=1, keepdims=True)
                mask_sc[n] = jnp.broadcast_to(jnp.where(cnt < MOBA_TOPK, 0.0, -jnp.inf), (t, LANES))
        for j in range(qi + 1):
            s = _dot_t(q, k_ref[0, j * t:(j + 1) * t, :])
            if j == qi:
                s = jnp.where(causal, s + bias_ref[0, 0], -jnp.inf)
            else:
                if j == qi - 1:
                    s = s + bias_ref[0, 1]
                if masked:
                    mk = mask_sc[j]
                    s = s + jnp.concatenate([mk] * (t // LANES), axis=1)
            s_sc[:, j * t:(j + 1) * t] = s
        l = _softmax_rows(s_sc, p_sc, n_cols, c2)
        o_ref[0, r0:r0 + t, :] = (_dot(p_sc[:, :n_cols], v_ref[0, :n_cols, :]) / l).astype(o_ref.dtype)


def _moba_attention(q, kv, bias_tiles, n_heads):
    b, s, w = q.shape
    dh = w // n_heads
    t = MOBA_BLOCK
    nb = s // t
    assert s % t == 0 and bias_tiles.shape[-1] == t and nb <= LANES
    kern = functools.partial(_moba_kernel, t=t)
    return pl.pallas_call(
        kern,
        grid=(b, n_heads),
        in_specs=[pl.BlockSpec((1, s, dh), lambda bi, h: (bi, 0, h)),
                  pl.BlockSpec((1, s, dh), lambda bi, h: (bi, 0, h)),
                  pl.BlockSpec((1, s, dh), lambda bi, h: (bi, 0, n_heads + h)),
                  pl.BlockSpec((1, 2, t, t), lambda bi, h: (h, 0, 0, 0))],
        out_specs=pl.BlockSpec((1, s, dh), lambda bi, h: (bi, 0, h)),
        out_shape=jax.ShapeDtypeStruct((b, s, w), BF16),
        scratch_shapes=[pltpu.VMEM((t, s), F32), pltpu.VMEM((t, s), BF16), pltpu.VMEM((nb, t, LANES), F32)],
        compiler_params=_cparams("parallel", "parallel"),
        name="moba_attn",
    )(q, kv, kv, bias_tiles)


def _top_experts(h_ref, w_ref, b_ref):
    aff = jax.nn.sigmoid(_dot(h_ref[...], w_ref[...]))
    work = aff + b_ref[...]
    lane = lax.broadcasted_iota(jnp.int32, aff.shape, 1).astype(F32)
    mask = jnp.zeros(aff.shape, F32)
    firsts, affs = [], []
    for _ in range(MOE_TOP_K):
        mx = jnp.max(work, axis=1, keepdims=True)
        first = jnp.min(jnp.where(work == mx, lane, float(LANES)), axis=1, keepdims=True)
        pick = lane == first
        affs.append(jnp.sum(jnp.where(pick, aff, 0.0), axis=1, keepdims=True))
        firsts.append(first)
        mask = jnp.where(pick, 1.0, mask)
        work = jnp.where(pick, -jnp.inf, work)
    return lane, mask, firsts, affs


def _router_count_kernel(h_ref, w_ref, b_ref, cnt_ref):
    @pl.when(pl.program_id(0) == 0)
    def _():
        cnt_ref[...] = jnp.zeros_like(cnt_ref)

    _, mask, _, _ = _top_experts(h_ref, w_ref, b_ref)
    cnt_ref[...] += jnp.broadcast_to(jnp.sum(mask, axis=0, keepdims=True), cnt_ref.shape)


def _router_kernel(h_ref, w_ref, b_ref, st_ref, cst_ref, g_ref, pos_ref, cpos_ref, carry_sc):
    @pl.when(pl.program_id(0) == 0)
    def _():
        carry_sc[...] = jnp.zeros_like(carry_sc)

    lane, mask, firsts, affs = _top_experts(h_ref, w_ref, b_ref)
    tm = mask.shape[0]
    denom = affs[0]
    for a in affs[1:]:
        denom = denom + a
    tri = jnp.where(lax.broadcasted_iota(jnp.int32, (tm, tm), 0) > lax.broadcasted_iota(jnp.int32, (tm, tm), 1),
                    1.0, 0.0).astype(BF16)
    carry = carry_sc[...]
    rank = _dot(tri, mask.astype(BF16)) + carry
    carry_sc[...] = carry + jnp.sum(mask, axis=0, keepdims=True)
    row = rank + st_ref[...]
    crow = rank + cst_ref[...]
    g_out = jnp.zeros(mask.shape, F32)
    p_out = jnp.zeros(mask.shape, F32)
    c_out = jnp.zeros(mask.shape, F32)
    for k in range(MOE_TOP_K):
        slot = lane == float(k)
        mine = lane == firsts[k]
        g_out = jnp.where(slot, affs[k] / denom * ROUTED_SCALE, g_out)
        p_out = jnp.where(slot, jnp.sum(jnp.where(mine, row, 0.0), axis=1, keepdims=True), p_out)
        c_out = jnp.where(slot, jnp.sum(jnp.where(mine, crow, 0.0), axis=1, keepdims=True), c_out)
    g_ref[...] = g_out
    pos_ref[...] = p_out.astype(jnp.int32)
    cpos_ref[...] = c_out.astype(jnp.int32)


def _route(h2d, router_w, router_b, tm_rows):
    t, d = h2d.shape
    n_exp = router_w.shape[1]
    assert n_exp <= LANES and MOE_TOP_K <= LANES and t * MOE_TOP_K + n_exp * tm_rows < 2 ** 24
    w = jnp.pad(router_w, ((0, 0), (0, LANES - n_exp))).astype(BF16)
    bias = jnp.pad(router_b, (0, LANES - n_exp), constant_values=-jnp.inf).reshape(1, LANES)
    tm = _tile(t, 256)
    tok = pl.BlockSpec((tm, LANES), lambda i: (i, 0))
    vec = pl.BlockSpec((1, LANES), lambda i: (0, 0))
    in_specs = [pl.BlockSpec((tm, d), lambda i: (i, 0)), pl.BlockSpec((d, LANES), lambda i: (0, 0)), vec]
    cnt = pl.pallas_call(
        _router_count_kernel,
        grid=(t // tm,),
        in_specs=in_specs,
        out_specs=pl.BlockSpec((SUBLANES, LANES), lambda i: (0, 0)),
        out_shape=jax.ShapeDtypeStruct((SUBLANES, LANES), F32),
        compiler_params=_cparams("arbitrary"),
        name="router_count",
    )(h2d, w, bias)
    counts = cnt[0, :n_exp].astype(jnp.int32)
    padded = (counts + tm_rows - 1) // tm_rows * tm_rows
    ends = jnp.cumsum(padded)
    starts = ends - padded
    cstarts = jnp.cumsum(counts) - counts

    def lanes(v):
        return jnp.pad(v.astype(F32), (0, LANES - n_exp)).reshape(1, LANES)

    gates, pos, cpos = pl.pallas_call(
        _router_kernel,
        grid=(t // tm,),
        in_specs=in_specs + [vec, vec],
        out_specs=[tok, tok, tok],
        out_shape=[jax.ShapeDtypeStruct((t, LANES), F32), jax.ShapeDtypeStruct((t, LANES), jnp.int32),
                   jax.ShapeDtypeStruct((t, LANES), jnp.int32)],
        scratch_shapes=[pltpu.VMEM((1, LANES), F32)],
        compiler_params=_cparams("arbitrary"),
        name="router",
    )(h2d, w, bias, lanes(starts), lanes(cstarts))
    pos = pos[:, :MOE_TOP_K].reshape(-1)
    token_of = jnp.arange(t * MOE_TOP_K, dtype=jnp.int32) // MOE_TOP_K
    _, sorted_tok = lax.sort_key_val(cpos[:, :MOE_TOP_K].reshape(-1), token_of)
    sorted_tok = jnp.concatenate([sorted_tok, jnp.zeros((tm_rows,), jnp.int32)])
    n_tiles = t * MOE_TOP_K // tm_rows + n_exp
    row0 = jnp.arange(n_tiles, dtype=jnp.int32) * tm_rows
    tile_expert = jnp.minimum(jnp.sum(ends[None, :] <= row0[:, None], axis=1), n_exp - 1).astype(jnp.int32)
    n_used = (ends[-1:] // tm_rows).astype(jnp.int32)
    src0 = cstarts[tile_expert] + row0 - starts[tile_expert]
    src0 = jnp.where(row0 < ends[-1], src0, 0).astype(jnp.int32)
    return gates, pos, sorted_tok, tile_expert, n_used, src0


def _experts_kernel(te_ref, nu_ref, src_ref, tok_ref, h_ref, wg_ref, wu_ref, wd_ref, y_ref,
                    xbuf, xb_sc, wgu_sc, wd_sc, sems):
    i = pl.program_id(0)
    tm = xbuf.shape[1]
    f = wg_ref.shape[1]
    n_used = nu_ref[0]
    live = i < n_used
    has_next = i + 1 < n_used
    slot = i % 2

    def row_copy(slot_, r, tok):
        return pltpu.make_async_copy(h_ref.at[pl.ds(tok, 1)], xbuf.at[slot_, pl.ds(r, 1)], sems.at[slot_])

    def issue(tile, slot_):
        s0 = src_ref[tile]
        for r in range(tm):
            row_copy(slot_, r, tok_ref[s0 + r]).start()

    def drain(slot_):
        for r in range(tm):
            row_copy(slot_, r, 0).wait()

    def stage():
        lo, hi = _unpack_halves(xbuf[slot])
        half = lo.shape[1]
        xb_sc[:, :half] = lo.astype(xb_sc.dtype)
        xb_sc[:, half:] = hi.astype(xb_sc.dtype)

    def compute():
        hgu = _dot(xb_sc[...], wgu_sc[...])
        hg = hgu[:, :f]
        hid = (hg * jax.nn.sigmoid(hg) * hgu[:, f:]).astype(wd_sc.dtype)
        y_ref[...] = _pack_halves(_dot(hid, wd_sc[...]))

    @pl.when(i == 0)
    def _():
        issue(0, 0)

    fresh = jnp.logical_or(i == 0, te_ref[i] != te_ref[jnp.maximum(i - 1, 0)])

    @pl.when(jnp.logical_and(live, fresh))
    def _():
        wgu_sc[:, :f] = wg_ref[...].astype(wgu_sc.dtype)
        wgu_sc[:, f:] = wu_ref[...].astype(wgu_sc.dtype)
        wd_sc[...] = wd_ref[...].astype(wd_sc.dtype)

    @pl.when(jnp.logical_and(live, has_next))
    def _():
        drain(slot)
        stage()
        issue(i + 1, 1 - slot)
        compute()

    @pl.when(jnp.logical_and(live, jnp.logical_not(has_next)))
    def _():
        drain(slot)
        stage()
        compute()

    @pl.when(jnp.logical_not(live))
    def _():
        y_ref[...] = jnp.zeros_like(y_ref)


def _experts(h2d, sorted_tok, tile_expert, n_used, src0, wg, wu, wd, layer, tm):
    t, dp = h2d.shape
    d = 2 * dp
    f = wg.shape[3]
    n_tiles = tile_expert.shape[0]
    return pl.pallas_call(
        _experts_kernel,
        grid_spec=pltpu.PrefetchScalarGridSpec(
            num_scalar_prefetch=4,
            grid=(n_tiles,),
            in_specs=[pl.BlockSpec(memory_space=pl.ANY),
                      pl.BlockSpec((None, None, d, f), lambda i, te, *_: (layer, te[i], 0, 0)),
                      pl.BlockSpec((None, None, d, f), lambda i, te, *_: (layer, te[i], 0, 0)),
                      pl.BlockSpec((None, None, f, d), lambda i, te, *_: (layer, te[i], 0, 0))],
            out_specs=pl.BlockSpec((tm, dp), lambda i, *_: (i, 0)),
            scratch_shapes=[pltpu.VMEM((2, tm, dp), jnp.uint32), pltpu.VMEM((tm, d), BF16),
                            pltpu.VMEM((d, 2 * f), BF16), pltpu.VMEM((f, d), BF16),
                            pltpu.SemaphoreType.DMA((2,))]),
        out_shape=jax.ShapeDtypeStruct((n_tiles * tm, dp), jnp.uint32),
        compiler_params=_cparams("arbitrary"),
        name="experts",
    )(tile_expert, n_used, src0, sorted_tok, h2d, wg, wu, wd)


def _combine_kernel(pos_ref, g_ref, y_ref, o_ref, buf, sems, *, tt):
    i = pl.program_id(0)
    slot = i % 2

    def row_copy(slot_, t, k, p):
        return pltpu.make_async_copy(y_ref.at[pl.ds(p, 1)], buf.at[slot_, k, pl.ds(t, 1)], sems.at[slot_])

    def issue(tile, slot_):
        base = tile * (tt * MOE_TOP_K)
        for t in range(tt):
            for k in range(MOE_TOP_K):
                row_copy(slot_, t, k, pos_ref[base + t * MOE_TOP_K + k]).start()

    def drain(slot_):
        for t in range(tt):
            for k in range(MOE_TOP_K):
                row_copy(slot_, t, k, 0).wait()

    @pl.when(i == 0)
    def _():
        issue(0, 0)

    @pl.when(i + 1 < pl.num_programs(0))
    def _():
        issue(i + 1, 1 - slot)

    drain(slot)
    g = g_ref[...]
    half = buf.shape[-1]
    acc_lo = acc_hi = None
    for k in range(MOE_TOP_K):
        lo, hi = _unpack_halves(buf[slot, k])
        gk = g[:, k:k + 1]
        acc_lo = gk * lo if acc_lo is None else acc_lo + gk * lo
        acc_hi = gk * hi if acc_hi is None else acc_hi + gk * hi
    o_ref[:, :half] = acc_lo
    o_ref[:, half:] = acc_hi


def _combine(y, pos, gates):
    t = gates.shape[0]
    dp = y.shape[1]
    d = 2 * dp
    tt = _tile(t, 64)
    kern = functools.partial(_combine_kernel, tt=tt)
    return pl.pallas_call(
        kern,
        grid_spec=pltpu.PrefetchScalarGridSpec(
            num_scalar_prefetch=1,
            grid=(t // tt,),
            in_specs=[pl.BlockSpec((tt, LANES), lambda i, *_: (i, 0)),
                      pl.BlockSpec(memory_space=pl.ANY)],
            out_specs=pl.BlockSpec((tt, d), lambda i, *_: (i, 0)),
            scratch_shapes=[pltpu.VMEM((2, MOE_TOP_K, tt, dp), jnp.uint32), pltpu.SemaphoreType.DMA((2,))]),
        out_shape=jax.ShapeDtypeStruct((t, d), F32),
        compiler_params=_cparams("arbitrary"),
        name="moe_combine",
    )(pos, gates, y)


def _shared_kernel(x_ref, wg_ref, wu_ref, wd_ref, o_ref, wgu_sc, wd_sc):
    f = wg_ref.shape[1]

    @pl.when(pl.program_id(0) == 0)
    def _():
        wgu_sc[:, :f] = wg_ref[...].astype(wgu_sc.dtype)
        wgu_sc[:, f:] = wu_ref[...].astype(wgu_sc.dtype)
        wd_sc[...] = wd_ref[...].astype(wd_sc.dtype)

    hgu = _dot(x_ref[...], wgu_sc[...])
    hg = hgu[:, :f]
    hid = (hg * jax.nn.sigmoid(hg) * hgu[:, f:]).astype(wd_sc.dtype)
    o_ref[...] = _dot(hid, wd_sc[...])


def _shared_expert(h2d, wg, wu, wd, layer):
    t, d = h2d.shape
    f = wg.shape[2]
    tm = _tile(t, 512)
    return pl.pallas_call(
        _shared_kernel,
        grid=(t // tm,),
        in_specs=[pl.BlockSpec((tm, d), lambda i: (i, 0)),
                  pl.BlockSpec((None, d, f), lambda i: (layer, 0, 0)),
                  pl.BlockSpec((None, d, f), lambda i: (layer, 0, 0)),
                  pl.BlockSpec((None, f, d), lambda i: (layer, 0, 0))],
        out_specs=pl.BlockSpec((tm, d), lambda i: (i, 0)),
        out_shape=jax.ShapeDtypeStruct((t, d), F32),
        scratch_shapes=[pltpu.VMEM((d, 2 * f), BF16), pltpu.VMEM((f, d), BF16)],
        compiler_params=_cparams("arbitrary"),
        name="shared_expert",
    )(h2d, wg, wu, wd)


EXPERT_ROW_TILE = 256


def _moe(h, h32, layer, router_w, router_b, wg, wu, wd, sg, su, sd):
    b, s, d = h.shape
    t = b * s
    tm = EXPERT_ROW_TILE
    h2d = h.reshape(t, d)
    gates, pos, sorted_tok, tile_expert, n_used, src0 = _route(h2d, router_w[layer], router_b[layer], tm)
    y = _experts(h32.reshape(t, d // 2), sorted_tok, tile_expert, n_used, src0, wg, wu, wd, layer, tm)
    routed = _combine(y, pos, gates)
    shared = _shared_expert(h2d, sg, su, sd, layer)
    return routed.reshape(b, s, d), shared.reshape(b, s, d)


def kernel(x, c, rel_bias, ada_w, ada_b, a_w_qkv, a_lambda, a_subln_g, a_w_o, kv_ada_w, kv_ada_b, kv_w,
           b_w_q, b_w_o, router_w, router_b, e_w_gate, e_w_up, e_w_down, s_w_gate, s_w_up, s_w_down, final_g):
    b, s, d = x.shape
    depth = ada_w.shape[0]
    n_a = a_w_qkv.shape[0]
    a_heads = a_w_qkv.shape[2] // (3 * 2 * A_HEAD_DIM)
    b_heads = rel_bias.shape[1]
    b_head_dim = b_w_q.shape[2] // b_heads
    assert 2 * a_heads == b_heads and 1 <= n_a < depth and b_head_dim == A_HEAD_DIM

    c_pad = jnp.pad(c, ((0, 8 - b), (0, 0)))

    def ada(w, layer, bias, parts):
        return _ada_proj(c_pad, w, layer, bias)[:b].reshape(b, parts, 1, d)

    bias_tiles = _bias_tiles(rel_bias, MOBA_BLOCK, s, A_HEAD_DIM ** -0.5)

    def vec(m, k):
        return m[:, k]

    mod = ada(ada_w, 0, ada_b[0], 6)
    (h,) = _resmod(x, [], None, [(vec(mod, 0), vec(mod, 1), True)], False, [BF16])
    kv = None
    for layer in range(depth):
        if layer < n_a:
            lambda_init = 0.8 - 0.6 * math.exp(-0.3 * layer)
            qkv = _matmul(h.reshape(b * s, d), a_w_qkv, layer, BF16)
            o = _diff_attention(qkv.reshape(b, s, -1), a_lambda[layer], a_subln_g[layer], bias_tiles,
                                a_heads, lambda_init)
            w_o, w_o_layer = a_w_o, layer
        else:
            j = layer - n_a
            hq = h
            if kv is None:
                hq, hkv = h
                kv = _matmul(hkv.reshape(b * s, d), kv_w[None], 0, BF16).reshape(b, s, -1)
            q = _matmul(hq.reshape(b * s, d), b_w_q, j, BF16).reshape(b, s, -1)
            o = _moba_attention(q, kv, bias_tiles, b_heads)
            w_o, w_o_layer = b_w_o, j
        mix = _matmul(o.reshape(b * s, -1), w_o, w_o_layer, F32).reshape(b, s, d)
        ffn_mod = (vec(mod, 3), vec(mod, 4), True)
        x, h, h32 = _resmod(x, [mix], vec(mod, 2), [ffn_mod, ffn_mod], True, [BF16, jnp.uint32])
        routed, shared = _moe(h, h32, layer, router_w, router_b, e_w_gate, e_w_up, e_w_down,
                              s_w_gate, s_w_up, s_w_down)
        g_ffn = vec(mod, 5)
        if layer + 1 == depth:
            zero = jnp.zeros((1, 1, d), F32)
            (out,) = _resmod(x, [routed, shared], g_ffn, [(zero, final_g.reshape(1, 1, d), False)], False, [F32])
            return out
        mod = ada(ada_w, layer + 1, ada_b[layer + 1], 6)
        mods = [(vec(mod, 0), vec(mod, 1), True)]
        if layer + 1 == n_a:
            kvmod = ada(kv_ada_w[None], 0, kv_ada_b, 2)
            mods.append((vec(kvmod, 0), vec(kvmod, 1), True))
            x, hq, hkv = _resmod(x, [routed, shared], g_ffn, mods, True, [BF16, BF16])
            h = (hq, hkv)
        else:
            x, h = _resmod(x, [routed, shared], g_ffn, mods, True, [BF16])
```
